```python
import math
import jax, jax.numpy as jnp
from jax import lax
import numpy as np

D_MODEL = 1024
BATCH = 4
SEQ = 4096
DEPTH = 1

SG_GROUPS = 8
SG_GROUP_DIM = 64
SG_WIDTH = SG_GROUPS * SG_GROUP_DIM
CHUNK = 128
DA_HEADS = 8
DA_HEAD_DIM = 64
DA_V_DIM = 2 * DA_HEAD_DIM
DA_QK_WIDTH = DA_HEADS * 2 * DA_HEAD_DIM
DA_WIDTH = DA_HEADS * DA_V_DIM
Q_BLOCK = 128
N_BRANCHES = 2
IN_COLS = 2 * SG_WIDTH + 2 * DA_QK_WIDTH + DA_WIDTH + N_BRANCHES * D_MODEL
SPLITS = (SG_WIDTH, 2 * SG_WIDTH, 2 * SG_WIDTH + DA_QK_WIDTH,
          2 * SG_WIDTH + 2 * DA_QK_WIDTH, 2 * SG_WIDTH + 2 * DA_QK_WIDTH + DA_WIDTH)
D_FF = -(-(8 * D_MODEL) // (3 * 256)) * 256
EPS = 1e-6

kernel_name = "hybrid_gmlp_diffattn_gated_encoder"


def lambda_init(layer_idx):
    return 0.8 - 0.6 * math.exp(-0.3 * layer_idx)


def rmsnorm(x, g):
    xf = x.astype(jnp.float32)
    y = xf * lax.rsqrt(jnp.mean(xf * xf, axis=-1, keepdims=True) + EPS)
    return (y * g.astype(jnp.float32)).astype(x.dtype)


def layernorm(x, g, b):
    xf = x.astype(jnp.float32)
    mu = jnp.mean(xf, axis=-1, keepdims=True)
    var = jnp.mean(jnp.square(xf - mu), axis=-1, keepdims=True)
    y = (xf - mu) * lax.rsqrt(var + EPS)
    return (y * g.astype(jnp.float32) + b.astype(jnp.float32)).astype(x.dtype)


def alibi_slopes(n_heads):
    return jnp.asarray(2.0 ** (-8.0 * (np.arange(n_heads) + 1) / n_heads), dtype=jnp.float32)


def spatial_gating(u, v, ln_g, ln_b, w_s, b_s):
    B, S, _ = u.shape
    u = jax.nn.gelu(u)
    v = layernorm(jax.nn.gelu(v), ln_g, ln_b)
    vc = v.reshape(B, S // CHUNK, CHUNK, SG_GROUPS, SG_GROUP_DIM)
    sv = jnp.einsum('gts,bcsge->bctge', w_s, vc) + b_s.T[:, :, None]
    return u * sv.reshape(B, S, SG_WIDTH)


def diff_attention(q, k, v, qn_g, kn_g, lq1, lk1, lq2, lk2, subln_g, lam_init):
    B, S, _ = q.shape
    H, dh = DA_HEADS, DA_HEAD_DIM
    q = rmsnorm(q.reshape(B, S, H, 2, dh), qn_g).transpose(0, 2, 3, 1, 4)
    k = rmsnorm(k.reshape(B, S, H, 2, dh), kn_g).transpose(0, 2, 3, 1, 4)
    v = v.reshape(B, S, H, DA_V_DIM).transpose(0, 2, 1, 3)
    f32 = jnp.float32
    lam = (jnp.exp(jnp.sum(lq1.astype(f32) * lk1.astype(f32)))
           - jnp.exp(jnp.sum(lq2.astype(f32) * lk2.astype(f32))) + lam_init)
    slopes = alibi_slopes(H)
    scale = 1.0 / math.sqrt(dh)
    kpos = jnp.arange(S, dtype=jnp.int32)
    n_blocks = S // Q_BLOCK

    def one_block(args):
        qb, start = args
        qpos = start + jnp.arange(Q_BLOCK, dtype=jnp.int32)
        s = jnp.einsum('bhcqd,bhckd->bhcqk', qb, k).astype(f32) * scale
        dist = jnp.abs(qpos[:, None] - kpos[None, :]).astype(f32)
        s = s - slopes[None, :, None, None, None] * dist
        p = jax.nn.softmax(s, axis=-1)
        a = p[:, :, 0] - lam * p[:, :, 1]
        return jnp.einsum('bhqk,bhkd->bhqd', a.astype(v.dtype), v)

    q_blocks = q.reshape(B, H, 2, n_blocks, Q_BLOCK, dh).transpose(3, 0, 1, 2, 4, 5)
    starts = jnp.arange(n_blocks, dtype=jnp.int32) * Q_BLOCK
    o = lax.map(one_block, (q_blocks, starts))
    o = o.transpose(1, 0, 3, 2, 4).reshape(B, S, H, DA_V_DIM)
    o = rmsnorm(o, subln_g) * (1.0 - lam_init)
    return o.reshape(B, S, DA_WIDTH)


def setup_inputs(seed: int = 0) -> dict:
    key = jax.random.key(seed)
    ks = jax.random.split(key, 24)
    nrm = lambda k, shape, s: jax.random.normal(k, shape, jnp.float32) * s
    L, D = DEPTH, D_MODEL
    return {
        "x": nrm(ks[0], (BATCH, SEQ, D), 1.0),
        "norm1_g": 1.0 + nrm(ks[1], (L, D), 0.02),
        "w_in": nrm(ks[2], (L, D, IN_COLS), D ** -0.5),
        "b_gate": nrm(ks[3], (L, N_BRANCHES * D), 0.02),
        "sg_ln_g": 1.0 + nrm(ks[4], (L, SG_WIDTH), 0.02),
        "sg_ln_b": nrm(ks[5], (L, SG_WIDTH), 0.02),
        "sg_w": nrm(ks[6], (L, SG_GROUPS, CHUNK, CHUNK), CHUNK ** -0.5),
        "sg_b": 1.0 + nrm(ks[7], (L, SG_GROUPS, CHUNK), 0.02),
        "q_norm_g": 1.0 + nrm(ks[8], (L, DA_HEAD_DIM), 0.02),
        "k_norm_g": 1.0 + nrm(ks[9], (L, DA_HEAD_DIM), 0.02),
        "lam_q1": nrm(ks[10], (L, DA_HEAD_DIM), 0.1),
        "lam_k1": nrm(ks[11], (L, DA_HEAD_DIM), 0.1),
        "lam_q2": nrm(ks[12], (L, DA_HEAD_DIM), 0.1),
        "lam_k2": nrm(ks[13], (L, DA_HEAD_DIM), 0.1),
        "subln_g": 1.0 + nrm(ks[14], (L, DA_V_DIM), 0.02),
        "w_proj_sg": nrm(ks[15], (L, SG_WIDTH, D), SG_WIDTH ** -0.5),
        "w_proj_da": nrm(ks[16], (L, DA_WIDTH, D), DA_WIDTH ** -0.5),
        "w_out": nrm(ks[17], (L, D, D), D ** -0.5),
        "norm2_g": 1.0 + nrm(ks[18], (L, D), 0.02),
        "w_ffn_gate": nrm(ks[19], (L, D, D_FF), D ** -0.5),
        "w_ffn_up": nrm(ks[20], (L, D, D_FF), D ** -0.5),
        "w_ffn_down": nrm(ks[21], (L, D_FF, D), D_FF ** -0.5),
    }


def reference(x, norm1_g, w_in, b_gate, sg_ln_g, sg_ln_b, sg_w, sg_b, q_norm_g, k_norm_g,
              lam_q1, lam_k1, lam_q2, lam_k2, subln_g, w_proj_sg, w_proj_da, w_out,
              norm2_g, w_ffn_gate, w_ffn_up, w_ffn_down):
    for l in range(DEPTH):
        xn = rmsnorm(x, norm1_g[l])
        proj = xn @ w_in[l]
        u, v_sg, q, k, v_da, gate_pre = jnp.split(proj, SPLITS, axis=-1)
        o_sg = spatial_gating(u, v_sg, sg_ln_g[l], sg_ln_b[l], sg_w[l], sg_b[l])
        o_da = diff_attention(q, k, v_da, q_norm_g[l], k_norm_g[l], lam_q1[l], lam_k1[l],
                              lam_q2[l], lam_k2[l], subln_g[l], lambda_init(l))
        y_sg = o_sg @ w_proj_sg[l]
        y_da = o_da @ w_proj_da[l]
        gates = jax.nn.sigmoid((gate_pre + b_gate[l]).astype(jnp.float32)).astype(x.dtype)
        g_sg, g_da = jnp.split(gates, N_BRANCHES, axis=-1)
        x = x + (g_sg * y_sg + g_da * y_da) @ w_out[l]
        hn = rmsnorm(x, norm2_g[l])
        x = x + (jax.nn.silu(hn @ w_ffn_gate[l]) * (hn @ w_ffn_up[l])) @ w_ffn_down[l]
    return x
```

```python
import functools
import math

import jax
import jax.numpy as jnp
from jax import lax
from jax.experimental import pallas as pl
from jax.experimental.pallas import tpu as pltpu

D_MODEL = 1024
SG_GROUPS = 8
SG_GROUP_DIM = 64
SG_WIDTH = SG_GROUPS * SG_GROUP_DIM
CHUNK = 128
DA_HEADS = 8
DA_HEAD_DIM = 64
DA_V_DIM = 2 * DA_HEAD_DIM
DA_WIDTH = DA_HEADS * DA_V_DIM
EPS = 1e-6
LAMBDA_INIT = 0.8 - 0.6 * math.exp(-0.3 * 0)
LOG2E = math.log2(math.e)

BF16 = jnp.bfloat16
F32 = jnp.float32

LANES = 128
MXU_DIM = 256
VMEM_LIMIT_BYTES = 56 * 1024 * 1024

TM_PROJ = 512
TQ = 512
TK = 1024
TM_MERGE = 512
TM_FFN = 512
FF_CHUNK = 256


def _dot(a, b):
    return jnp.dot(a, b, preferred_element_type=F32)


def _const_spec(shape):
    n = len(shape)
    return pl.BlockSpec(shape, lambda *_: (0,) * n, pipeline_mode=pl.Buffered(1))


def _group_mean_sq(p):
    r = lax.broadcasted_iota(jnp.int32, (MXU_DIM, MXU_DIM), 0) // DA_HEAD_DIM
    c = lax.broadcasted_iota(jnp.int32, (MXU_DIM, MXU_DIM), 1) // DA_HEAD_DIM
    ones_bd = jnp.where(r == c, 1.0, 0.0).astype(BF16)
    sq = (p * p).astype(BF16)
    parts = [_dot(sq[:, i * MXU_DIM:(i + 1) * MXU_DIM], ones_bd) for i in range(p.shape[1] // MXU_DIM)]
    return jnp.concatenate(parts, axis=1) * (1.0 / DA_HEAD_DIM)


def _inproj_kernel(x_ref, n1g_ref, w_ref, bg_ref, lng_ref, lnb_ref, qg_ref, kg_ref,
                   gu_ref, vn_ref, q_ref, kt_ref, v_ref, gsg_ref, gda_ref):
    x = x_ref[...]
    ms = jnp.mean(x * x, axis=-1, keepdims=True)
    xn = (x * lax.rsqrt(ms + EPS) * n1g_ref[...]).astype(BF16)

    p = _dot(xn, w_ref[:, 0:2 * SG_WIDTH])
    gu_ref[...] = jax.nn.gelu(p[:, :SG_WIDTH]).astype(BF16)
    gv = jax.nn.gelu(p[:, SG_WIDTH:])
    mu = jnp.mean(gv, axis=-1, keepdims=True)
    cen = gv - mu
    var = jnp.mean(cen * cen, axis=-1, keepdims=True)
    vn_ref[...] = (cen * lax.rsqrt(var + EPS) * lng_ref[...] + lnb_ref[...]).astype(BF16)

    c0 = 2 * SG_WIDTH
    p = _dot(xn, w_ref[:, c0:c0 + DA_WIDTH])
    qn = (p * lax.rsqrt(_group_mean_sq(p) + EPS) * qg_ref[...]).astype(BF16)
    for h in range(DA_HEADS):
        q_ref[h] = qn[:, h * DA_V_DIM:(h + 1) * DA_V_DIM]

    c0 += DA_WIDTH
    p = _dot(xn, w_ref[:, c0:c0 + DA_WIDTH])
    kn = p * lax.rsqrt(_group_mean_sq(p) + EPS) * kg_ref[...]
    for h in range(DA_HEADS):
        kt_ref[h] = kn[:, h * DA_V_DIM:(h + 1) * DA_V_DIM].T.astype(BF16)

    c0 += DA_WIDTH
    p = _dot(xn, w_ref[:, c0:c0 + DA_WIDTH]).astype(BF16)
    for h in range(DA_HEADS):
        v_ref[h] = p[:, h * DA_V_DIM:(h + 1) * DA_V_DIM]

    c0 += DA_WIDTH
    p = _dot(xn, w_ref[:, c0:c0 + D_MODEL])
    gsg_ref[...] = jax.nn.sigmoid(p + bg_ref[:, :D_MODEL]).astype(BF16)
    c0 += D_MODEL
    p = _dot(xn, w_ref[:, c0:c0 + D_MODEL])
    gda_ref[...] = jax.nn.sigmoid(p + bg_ref[:, D_MODEL:]).astype(BF16)


def _in_projection(x, n1g, w_in, b_gate, ln_g, ln_b, qg, kg):
    B, S, D = x.shape
    tm = TM_PROJ
    n_t = S // tm
    n_kc = S // TK
    sub = TK // tm
    in_cols = w_in.shape[1]
    tok = lambda width: pl.BlockSpec((None, tm, width), lambda b, i: (b, i, 0))
    head = pl.BlockSpec((None, DA_HEADS, tm, DA_V_DIM), lambda b, i: (b, 0, i, 0))
    out_shape = (
        jax.ShapeDtypeStruct((B, S, SG_WIDTH), BF16),
        jax.ShapeDtypeStruct((B, S, SG_WIDTH), BF16),
        jax.ShapeDtypeStruct((B, DA_HEADS, S, DA_V_DIM), BF16),
        jax.ShapeDtypeStruct((B, DA_HEADS, n_kc, DA_V_DIM, TK), BF16),
        jax.ShapeDtypeStruct((B, DA_HEADS, S, DA_V_DIM), BF16),
        jax.ShapeDtypeStruct((B, S, D_MODEL), BF16),
        jax.ShapeDtypeStruct((B, S, D_MODEL), BF16),
    )
    out_specs = (
        tok(SG_WIDTH), tok(SG_WIDTH), head,
        pl.BlockSpec((None, DA_HEADS, None, DA_V_DIM, tm), lambda b, i: (b, 0, i // sub, 0, i % sub)),
        head, tok(D_MODEL), tok(D_MODEL),
    )
    in_specs = [
        tok(D),
        _const_spec((1, D)),
        _const_spec((D, in_cols)),
        _const_spec((1, 2 * D_MODEL)),
        _const_spec((1, SG_WIDTH)),
        _const_spec((1, SG_WIDTH)),
        _const_spec((1, DA_WIDTH)),
        _const_spec((1, DA_WIDTH)),
    ]
    return pl.pallas_call(
        _inproj_kernel,
        grid=(B, n_t),
        in_specs=in_specs,
        out_specs=out_specs,
        out_shape=out_shape,
        compiler_params=pltpu.CompilerParams(vmem_limit_bytes=VMEM_LIMIT_BYTES),
        name="in_projection",
    )(x, n1g, w_in, b_gate, ln_g, ln_b, qg, kg)


def _attn_kernel(slopes_ref, q_ref, kt_ref, v_ref, lq1_ref, lk1_ref, lq2_ref, lk2_ref, sg_ref, o_ref):
    h = pl.program_id(1)
    i = pl.program_id(2)
    tq = q_ref.shape[0]
    n_kc, _, tk = kt_ref.shape
    slope = slopes_ref[h]

    q = q_ref[...]
    lane = lax.broadcasted_iota(jnp.int32, q.shape, 1)
    zero = jnp.zeros_like(q)
    qs = jnp.concatenate([jnp.where(lane < DA_HEAD_DIM, q, zero),
                          jnp.where(lane >= DA_HEAD_DIM, q, zero)], axis=0)

    rel = (lax.broadcasted_iota(jnp.int32, (tq, tk), 0)
           - lax.broadcasted_iota(jnp.int32, (tq, tk), 1)).astype(F32) * slope
    rel = jnp.concatenate([rel, rel], axis=0)

    def body(j, carry):
        m, l, acc = carry
        off = (i * tq - j * tk).astype(F32) * slope
        s = _dot(qs, kt_ref[j]) - jnp.abs(rel + off)
        m_new = jnp.maximum(m, jnp.max(s, axis=-1, keepdims=True))
        alpha = jnp.exp2(m - m_new)
        p = jnp.exp2(s - m_new)
        l = alpha * l + jnp.sum(p, axis=-1, keepdims=True)
        vj = v_ref[pl.ds(pl.multiple_of(j * tk, tk), tk), :]
        acc = alpha * acc + _dot(p.astype(BF16), vj)
        return m_new, l, acc

    m0 = jnp.full((2 * tq, 1), -jnp.inf, F32)
    l0 = jnp.zeros((2 * tq, 1), F32)
    a0 = jnp.zeros((2 * tq, DA_V_DIM), F32)
    _, l, acc = lax.fori_loop(0, n_kc, body, (m0, l0, a0))

    lam = (jnp.exp(jnp.sum(lq1_ref[...] * lk1_ref[...], axis=-1, keepdims=True))
           - jnp.exp(jnp.sum(lq2_ref[...] * lk2_ref[...], axis=-1, keepdims=True)) + LAMBDA_INIT)
    o = acc / l
    o = o[:tq] - lam * o[tq:]
    ms = jnp.mean(o * o, axis=-1, keepdims=True)
    o_ref[...] = (o * lax.rsqrt(ms + EPS) * sg_ref[...] * (1.0 - LAMBDA_INIT)).astype(BF16)


def _attention(slopes, q, kt, v, lq1, lk1, lq2, lk2, subln_g):
    B, H, S, dv = q.shape
    n_kc = kt.shape[2]
    small = lambda n: pl.BlockSpec((1, n), lambda b, h, i, *_: (0, 0))
    grid_spec = pltpu.PrefetchScalarGridSpec(
        num_scalar_prefetch=1,
        grid=(B, H, S // TQ),
        in_specs=[
            pl.BlockSpec((None, None, TQ, dv), lambda b, h, i, *_: (b, h, i, 0)),
            pl.BlockSpec((None, None, n_kc, dv, TK), lambda b, h, i, *_: (b, h, 0, 0, 0)),
            pl.BlockSpec((None, None, S, dv), lambda b, h, i, *_: (b, h, 0, 0)),
            small(DA_HEAD_DIM), small(DA_HEAD_DIM), small(DA_HEAD_DIM), small(DA_HEAD_DIM),
            small(DA_V_DIM),
        ],
        out_specs=pl.BlockSpec((None, TQ, dv), lambda b, h, i, *_: (b, i, h)),
    )
    return pl.pallas_call(
        _attn_kernel,
        grid_spec=grid_spec,
        out_shape=jax.ShapeDtypeStruct((B, S, H * dv), BF16),
        compiler_params=pltpu.CompilerParams(vmem_limit_bytes=VMEM_LIMIT_BYTES),
        name="diff_attention",
    )(slopes, q, kt, v, lq1, lk1, lq2, lk2, subln_g)


def _merge_kernel(x_ref, gu_ref, vn_ref, oda_ref, gsg_ref, gda_ref, sw_ref, sb_ref,
                  psg_ref, pda_ref, wo_ref, h_ref):
    tm = x_ref.shape[0]
    lane = lax.broadcasted_iota(jnp.int32, (CHUNK, LANES), 1)
    first_group = lane < SG_GROUP_DIM
    bias = sb_ref[...]
    rows = []
    for c in range(tm // CHUNK):
        vc = vn_ref[c * CHUNK:(c + 1) * CHUNK, :]
        pairs = []
        for pr in range(SG_GROUPS // 2):
            vp = vc[:, pr * LANES:(pr + 1) * LANES]
            pairs.append(jnp.where(first_group, _dot(sw_ref[2 * pr], vp), _dot(sw_ref[2 * pr + 1], vp)))
        sv = jnp.concatenate(pairs, axis=1) + bias
        rows.append((gu_ref[c * CHUNK:(c + 1) * CHUNK, :].astype(F32) * sv).astype(BF16))
    o_sg = jnp.concatenate(rows, axis=0)
    y_sg = _dot(o_sg, psg_ref[...])
    y_da = _dot(oda_ref[...], pda_ref[...])
    z = (gsg_ref[...].astype(F32) * y_sg + gda_ref[...].astype(F32) * y_da).astype(BF16)
    h_ref[...] = x_ref[...] + _dot(z, wo_ref[...])


def _merge(x, gu, vn, o_da, g_sg, g_da, sg_w, sg_bias, p_sg, p_da, w_out):
    B, S, D = x.shape
    tm = TM_MERGE
    tok = lambda width: pl.BlockSpec((None, tm, width), lambda b, i: (b, i, 0))
    return pl.pallas_call(
        _merge_kernel,
        grid=(B, S // tm),
        in_specs=[
            tok(D), tok(SG_WIDTH), tok(SG_WIDTH), tok(DA_WIDTH), tok(D_MODEL), tok(D_MODEL),
            _const_spec(sg_w.shape), _const_spec(sg_bias.shape),
            _const_spec(p_sg.shape), _const_spec(p_da.shape), _const_spec(w_out.shape),
        ],
        out_specs=tok(D),
        out_shape=jax.ShapeDtypeStruct((B, S, D), F32),
        compiler_params=pltpu.CompilerParams(vmem_limit_bytes=VMEM_LIMIT_BYTES),
        name="merge",
    )(x, gu, vn, o_da, g_sg, g_da, sg_w, sg_bias, p_sg, p_da, w_out)


def _ffn_kernel(h_ref, g_ref, wg_ref, wu_ref, wd_ref, o_ref):
    hres = h_ref[...]
    ms = jnp.mean(hres * hres, axis=-1, keepdims=True)
    hn = (hres * lax.rsqrt(ms + EPS) * g_ref[...]).astype(BF16)
    d_ff = wg_ref.shape[1]
    acc = hres
    for c in range(d_ff // FF_CHUNK):
        sl = slice(c * FF_CHUNK, (c + 1) * FF_CHUNK)
        a = jax.nn.silu(_dot(hn, wg_ref[:, sl])) * _dot(hn, wu_ref[:, sl])
        acc = acc + _dot(a.astype(BF16), wd_ref[sl, :])
    o_ref[...] = acc


def _ffn(h, g, w_gate, w_up, w_down):
    B, S, D = h.shape
    tm = TM_FFN
    tok = pl.BlockSpec((None, tm, D), lambda b, i: (b, i, 0))
    return pl.pallas_call(
        _ffn_kernel,
        grid=(B, S // tm),
        in_specs=[tok, _const_spec((1, D)), _const_spec(w_gate.shape), _const_spec(w_up.shape),
                  _const_spec(w_down.shape)],
        out_specs=tok,
        out_shape=jax.ShapeDtypeStruct((B, S, D), F32),
        compiler_params=pltpu.CompilerParams(vmem_limit_bytes=VMEM_LIMIT_BYTES),
        name="swiglu_ffn",
    )(h, g, w_gate, w_up, w_down)


def kernel(x, norm1_g, w_in, b_gate, sg_ln_g, sg_ln_b, sg_w, sg_b, q_norm_g, k_norm_g, lam_q1, lam_k1, lam_q2, lam_k2, subln_g, w_proj_sg, w_proj_da, w_out, norm2_g, w_ffn_gate, w_ffn_up, w_ffn_down):
    depth = w_in.shape[0]
    row = lambda a: a.reshape(1, -1).astype(F32)
    n_rep = DA_WIDTH // DA_HEAD_DIM
    slopes = jnp.asarray([2.0 ** (-8.0 * (i + 1) / DA_HEADS) * LOG2E for i in range(DA_HEADS)], F32)
    for l in range(depth):
        qg = row(jnp.tile(q_norm_g[l], n_rep)) * (LOG2E / math.sqrt(DA_HEAD_DIM))
        kg = row(jnp.tile(k_norm_g[l], n_rep))
        gu, vn, q, kt, v, g_sg, g_da = _in_projection(
            x, row(norm1_g[l]), w_in[l].astype(BF16), row(b_gate[l]),
            row(sg_ln_g[l]), row(sg_ln_b[l]), qg, kg)
        o_da = _attention(slopes, q, kt, v, row(lam_q1[l]), row(lam_k1[l]), row(lam_q2[l]),
                          row(lam_k2[l]), row(subln_g[l]))
        sg_bias = jnp.repeat(sg_b[l].T.astype(F32), SG_GROUP_DIM, axis=1)
        h = _merge(x, gu, vn, o_da, g_sg, g_da, sg_w[l].astype(BF16), sg_bias,
                   w_proj_sg[l].astype(BF16), w_proj_da[l].astype(BF16), w_out[l].astype(BF16))
        x = _ffn(h, row(norm2_g[l]), w_ffn_gate[l].astype(BF16), w_ffn_up[l].astype(BF16),
                 w_ffn_down[l].astype(BF16))
    return x
```

```python
import functools
import math

import jax
import jax.numpy as jnp
from jax import lax
from jax.experimental import pallas as pl
from jax.experimental.pallas import tpu as pltpu

D_MODEL = 1024
SG_GROUPS = 8
SG_GROUP_DIM = 64
SG_WIDTH = SG_GROUPS * SG_GROUP_DIM
CHUNK = 128
DA_HEADS = 8
DA_HEAD_DIM = 64
DA_V_DIM = 2 * DA_HEAD_DIM
DA_WIDTH = DA_HEADS * DA_V_DIM
EPS = 1e-6
LAMBDA_INIT = 0.8 - 0.6 * math.exp(-0.3 * 0)
LOG2E = math.log2(math.e)

BF16 = jnp.bfloat16
F32 = jnp.float32

LANES = 128
MXU_DIM = 256
VMEM_LIMIT_BYTES = 56 * 1024 * 1024

TM_PROJ = 512
TQ = 512
TK = 512
POS_SPLIT = 64
N_AUG = 15
SAFE_EXP2_RANGE = 60.0
assert TQ == TK
TM_MERGE = 512
TM_FFN = 512
FF_CHUNK = 256


def _dot(a, b):
    return jnp.dot(a, b, preferred_element_type=F32)


def _const_spec(shape):
    n = len(shape)
    return pl.BlockSpec(shape, lambda *_: (0,) * n, pipeline_mode=pl.Buffered(1))


def _group_mean_sq(p):
    r = lax.broadcasted_iota(jnp.int32, (MXU_DIM, MXU_DIM), 0) // DA_HEAD_DIM
    c = lax.broadcasted_iota(jnp.int32, (MXU_DIM, MXU_DIM), 1) // DA_HEAD_DIM
    ones_bd = jnp.where(r == c, 1.0, 0.0).astype(BF16)
    sq = (p * p).astype(BF16)
    parts = [_dot(sq[:, i * MXU_DIM:(i + 1) * MXU_DIM], ones_bd) for i in range(p.shape[1] // MXU_DIM)]
    return jnp.concatenate(parts, axis=1) * (1.0 / DA_HEAD_DIM)


def _inproj_kernel(x_ref, n1g_ref, w_ref, bg_ref, lng_ref, lnb_ref, qg_ref, kg_ref,
                   gu_ref, vn_ref, q_ref, kt_ref, v_ref, gsg_ref, gda_ref):
    x = x_ref[...]
    ms = jnp.mean(x * x, axis=-1, keepdims=True)
    xn = (x * lax.rsqrt(ms + EPS) * n1g_ref[...]).astype(BF16)

    p = _dot(xn, w_ref[:, 0:2 * SG_WIDTH])
    gu_ref[...] = jax.nn.gelu(p[:, :SG_WIDTH]).astype(BF16)
    gv = jax.nn.gelu(p[:, SG_WIDTH:])
    mu = jnp.mean(gv, axis=-1, keepdims=True)
    cen = gv - mu
    var = jnp.mean(cen * cen, axis=-1, keepdims=True)
    vn_ref[...] = (cen * lax.rsqrt(var + EPS) * lng_ref[...] + lnb_ref[...]).astype(BF16)

    c0 = 2 * SG_WIDTH
    p = _dot(xn, w_ref[:, c0:c0 + DA_WIDTH])
    qn = (p * lax.rsqrt(_group_mean_sq(p) + EPS) * qg_ref[...]).astype(BF16)
    for h in range(DA_HEADS):
        q_ref[h] = qn[:, h * DA_V_DIM:(h + 1) * DA_V_DIM]

    c0 += DA_WIDTH
    p = _dot(xn, w_ref[:, c0:c0 + DA_WIDTH])
    kn = p * lax.rsqrt(_group_mean_sq(p) + EPS) * kg_ref[...]
    for h in range(DA_HEADS):
        kt_ref[h] = kn[:, h * DA_V_DIM:(h + 1) * DA_V_DIM].T.astype(BF16)

    c0 += DA_WIDTH
    p = _dot(xn, w_ref[:, c0:c0 + DA_WIDTH]).astype(BF16)
    for h in range(DA_HEADS):
        v_ref[h] = p[:, h * DA_V_DIM:(h + 1) * DA_V_DIM]

    c0 += DA_WIDTH
    p = _dot(xn, w_ref[:, c0:c0 + D_MODEL])
    gsg_ref[...] = jax.nn.sigmoid(p + bg_ref[:, :D_MODEL]).astype(BF16)
    c0 += D_MODEL
    p = _dot(xn, w_ref[:, c0:c0 + D_MODEL])
    gda_ref[...] = jax.nn.sigmoid(p + bg_ref[:, D_MODEL:]).astype(BF16)


def _in_projection(x, n1g, w_in, b_gate, ln_g, ln_b, qg, kg):
    B, S, D = x.shape
    tm = TM_PROJ
    n_t = S // tm
    n_kc = S // TK
    sub = TK // tm
    in_cols = w_in.shape[1]
    tok = lambda width: pl.BlockSpec((None, tm, width), lambda b, i: (b, i, 0))
    head = pl.BlockSpec((None, DA_HEADS, tm, DA_V_DIM), lambda b, i: (b, 0, i, 0))
    out_shape = (
        jax.ShapeDtypeStruct((B, S, SG_WIDTH), BF16),
        jax.ShapeDtypeStruct((B, S, SG_WIDTH), BF16),
        jax.ShapeDtypeStruct((B, DA_HEADS, S, DA_V_DIM), BF16),
        jax.ShapeDtypeStruct((B, DA_HEADS, n_kc, DA_V_DIM, TK), BF16),
        jax.ShapeDtypeStruct((B, DA_HEADS, S, DA_V_DIM), BF16),
        jax.ShapeDtypeStruct((B, S, D_MODEL), BF16),
        jax.ShapeDtypeStruct((B, S, D_MODEL), BF16),
    )
    out_specs = (
        tok(SG_WIDTH), tok(SG_WIDTH), head,
        pl.BlockSpec((None, DA_HEADS, None, DA_V_DIM, tm), lambda b, i: (b, 0, i // sub, 0, i % sub)),
        head, tok(D_MODEL), tok(D_MODEL),
    )
    in_specs = [
        tok(D),
        _const_spec((1, D)),
        _const_spec((D, in_cols)),
        _const_spec((1, 2 * D_MODEL)),
        _const_spec((1, SG_WIDTH)),
        _const_spec((1, SG_WIDTH)),
        _const_spec((1, DA_WIDTH)),
        _const_spec((1, DA_WIDTH)),
    ]
    return pl.pallas_call(
        _inproj_kernel,
        grid=(B, n_t),
        in_specs=in_specs,
        out_specs=out_specs,
        out_shape=out_shape,
        compiler_params=pltpu.CompilerParams(vmem_limit_bytes=VMEM_LIMIT_BYTES),
        name="in_projection",
    )(x, n1g, w_in, b_gate, ln_g, ln_b, qg, kg)


def _split3(x):
    hi = x.astype(BF16).astype(F32)
    r = x - hi
    mid = r.astype(BF16).astype(F32)
    lo = (r - mid).astype(BF16).astype(F32)
    return hi, mid, lo


def _alibi_constants(slopes, n_kc, tk):
    parts = jnp.stack(_split3(slopes), axis=0)
    idx = jnp.arange(LANES, dtype=jnp.int32)
    part_of = parts[jnp.clip(idx, 0, N_AUG - 1) % 3]
    srow = jnp.where(((idx >= 9) & (idx < N_AUG))[:, None], part_of, 0.0).T
    pos = jnp.arange(n_kc * tk, dtype=jnp.int32).reshape(1, n_kc, 1, tk)
    row = idx.reshape(1, 1, LANES, 1)
    coarse = ((pos // POS_SPLIT) * POS_SPLIT).astype(F32)
    fine = (pos % POS_SPLIT).astype(F32)
    neg_slope = -part_of.T.reshape(-1, 1, LANES, 1)
    rows = jnp.where(row < 3, -1.0, jnp.where(row < 9, neg_slope,
           jnp.where(row < 12, coarse, jnp.where(row < N_AUG, fine, 0.0))))
    return rows.astype(BF16), srow.reshape(-1, 1, LANES)


def _attn_kernel(slopes_ref, q_ref, kt_ref, kpos_ref, srow_ref, qtab_ref, dist_ref, v_ref,
                 lq1_ref, lk1_ref, lq2_ref, lk2_ref, sg_ref, o_ref, qaug_ref, acc_ref, kmax_ref):
    h = pl.program_id(1)
    i = pl.program_id(2)
    tq = q_ref.shape[0]
    n_kc, _, tk = kt_ref.shape
    slope = slopes_ref[h]

    @pl.when(i == 0)
    def _():
        best1 = jnp.zeros((1, tk), F32)
        best2 = jnp.zeros((1, tk), F32)
        for c in range(n_kc):
            kf = kt_ref[c].astype(F32)
            sq = kf * kf
            best1 = jnp.maximum(best1, jnp.sum(sq[:DA_HEAD_DIM], axis=0, keepdims=True))
            best2 = jnp.maximum(best2, jnp.sum(sq[DA_HEAD_DIM:], axis=0, keepdims=True))
        kmax_ref[0:1, :] = jnp.broadcast_to(jnp.sqrt(jnp.max(best1, axis=1, keepdims=True)), (1, LANES))
        kmax_ref[1:2, :] = jnp.broadcast_to(jnp.sqrt(jnp.max(best2, axis=1, keepdims=True)), (1, LANES))

    rb = CHUNK
    lane = lax.broadcasted_iota(jnp.int32, (rb, LANES), 1)
    first = lane < DA_HEAD_DIM
    block_base = jnp.where((lane >= 3) & (lane < 6), (i * tq).astype(F32), 0.0) + srow_ref[...]
    m_max = jnp.zeros((rb, LANES), F32)
    for r in range(tq // rb):
        rows = slice(r * rb, (r + 1) * rb)
        q = q_ref[rows, :]
        qsq = q.astype(F32) * q.astype(F32)
        positional = qtab_ref[rows, :] + block_base
        for comp in range(2):
            mine = first if comp == 0 else jnp.logical_not(first)
            norm = jnp.sqrt(jnp.sum(jnp.where(mine, qsq, 0.0), axis=-1, keepdims=True))
            m = norm * kmax_ref[comp:comp + 1, :]
            m_max = jnp.maximum(m_max, m)
            m_hi, m_mid, m_lo = _split3(m)
            m_parts = jnp.where(lane == 0, m_hi, jnp.where(lane == 1, m_mid, m_lo))
            qm = jnp.where(mine, q, jnp.zeros_like(q))
            out_rows = slice(comp * tq + r * rb, comp * tq + (r + 1) * rb)
            for variant, tail in enumerate((positional, -positional, jnp.zeros_like(positional))):
                qaug_ref[variant, out_rows, 0:LANES] = qm
                qaug_ref[variant, out_rows, LANES:2 * LANES] = jnp.where(lane < 3, m_parts, tail).astype(BF16)

    def v_chunk(j):
        return v_ref[pl.ds(pl.multiple_of(j * tk, tk), tk), :]

    fast = jnp.max(m_max) <= SAFE_EXP2_RANGE

    @pl.when(fast)
    def _():
        ones_col = jnp.where(lax.broadcasted_iota(jnp.int32, (tk, LANES), 1) == 0, 1.0, 0.0).astype(BF16)

        def chunk(j, qa, bias):
            e = _dot(qa, jnp.concatenate([kt_ref[j], kpos_ref[j]], axis=0))
            if bias is not None:
                e = e - bias
            p = jnp.exp2(e).astype(BF16)
            return _dot(p, jnp.concatenate([v_chunk(j), ones_col], axis=1))

        diag_bias = dist_ref[...] * slope
        acc = chunk(i, qaug_ref[2], jnp.concatenate([diag_bias, diag_bias], axis=0))
        for t in range(n_kc - 1):
            j = t + (t >= i).astype(jnp.int32)
            acc = acc + chunk(j, qaug_ref[(j > i).astype(jnp.int32)], None)
        acc_ref[...] = acc

    @pl.when(jnp.logical_not(fast))
    def _():
        signed = (lax.broadcasted_iota(jnp.int32, (2 * tq, tk), 0) % tq
                  - lax.broadcasted_iota(jnp.int32, (2 * tq, tk), 1)).astype(F32) * slope

        def body(j, carry):
            mx, l, acc = carry
            off = (i * tq - j * tk).astype(F32) * slope
            s = _dot(qaug_ref[2, :, 0:LANES], kt_ref[j]) - jnp.abs(signed + off)
            m_new = jnp.maximum(mx, jnp.max(s, axis=-1, keepdims=True))
            alpha = jnp.exp2(mx - m_new)
            p = jnp.exp2(s - m_new)
            l = alpha * l + jnp.sum(p, axis=-1, keepdims=True)
            acc = alpha * acc + _dot(p.astype(BF16), v_chunk(j))
            return m_new, l, acc

        m0 = jnp.full((2 * tq, 1), -jnp.inf, F32)
        l0 = jnp.zeros((2 * tq, 1), F32)
        a0 = jnp.zeros((2 * tq, DA_V_DIM), F32)
        _, l, acc = lax.fori_loop(0, n_kc, body, (m0, l0, a0))
        acc_ref[:, 0:LANES] = acc
        acc_ref[:, LANES:2 * LANES] = jnp.broadcast_to(l, (2 * tq, LANES))

    lam = (jnp.exp(jnp.sum(lq1_ref[...] * lk1_ref[...], axis=-1, keepdims=True))
           - jnp.exp(jnp.sum(lq2_ref[...] * lk2_ref[...], axis=-1, keepdims=True)) + LAMBDA_INIT)
    o = acc_ref[:, 0:LANES] / acc_ref[:, LANES:LANES + 1]
    o = o[:tq] - lam * o[tq:]
    ms = jnp.mean(o * o, axis=-1, keepdims=True)
    o_ref[...] = (o * lax.rsqrt(ms + EPS) * sg_ref[...] * (1.0 - LAMBDA_INIT)).astype(BF16)


def _attention(slopes, q, kt, v, lq1, lk1, lq2, lk2, subln_g):
    B, H, S, dv = q.shape
    n_kc = kt.shape[2]
    kpos, srow = _alibi_constants(slopes, n_kc, TK)
    r = jnp.arange(TQ, dtype=jnp.int32)[:, None]
    c = jnp.arange(LANES, dtype=jnp.int32)[None, :]
    qtab = jnp.where((c >= 3) & (c < 6), (r // POS_SPLIT) * POS_SPLIT,
                     jnp.where((c >= 6) & (c < 9), r % POS_SPLIT, 0)).astype(F32)
    dist = jnp.abs(r - jnp.arange(TK, dtype=jnp.int32)[None, :]).astype(F32)
    small = lambda n: pl.BlockSpec((1, n), lambda b, h, i, *_: (0, 0))
    grid_spec = pltpu.PrefetchScalarGridSpec(
        num_scalar_prefetch=1,
        grid=(B, H, S // TQ),
        in_specs=[
            pl.BlockSpec((None, None, TQ, dv), lambda b, h, i, *_: (b, h, i, 0)),
            pl.BlockSpec((None, None, n_kc, dv, TK), lambda b, h, i, *_: (b, h, 0, 0, 0)),
            pl.BlockSpec((None, n_kc, LANES, TK), lambda b, h, i, *_: (h, 0, 0, 0)),
            pl.BlockSpec((None, 1, LANES), lambda b, h, i, *_: (h, 0, 0)),
            pl.BlockSpec((TQ, LANES), lambda b, h, i, *_: (0, 0)),
            pl.BlockSpec((TQ, TK), lambda b, h, i, *_: (0, 0)),
            pl.BlockSpec((None, None, S, dv), lambda b, h, i, *_: (b, h, 0, 0)),
            small(DA_HEAD_DIM), small(DA_HEAD_DIM), small(DA_HEAD_DIM), small(DA_HEAD_DIM),
            small(DA_V_DIM),
        ],
        out_specs=pl.BlockSpec((None, TQ, dv), lambda b, h, i, *_: (b, i, h)),
        scratch_shapes=[
            pltpu.VMEM((3, 2 * TQ, 2 * LANES), BF16),
            pltpu.VMEM((2 * TQ, 2 * LANES), F32),
            pltpu.VMEM((8, LANES), F32),
        ],
    )
    return pl.pallas_call(
        _attn_kernel,
        grid_spec=grid_spec,
        out_shape=jax.ShapeDtypeStruct((B, S, H * dv), BF16),
        compiler_params=pltpu.CompilerParams(vmem_limit_bytes=VMEM_LIMIT_BYTES),
        name="diff_attention",
    )(slopes, q, kt, kpos, srow, qtab, dist, v, lq1, lk1, lq2, lk2, subln_g)


def _merge_kernel(x_ref, gu_ref, vn_ref, oda_ref, gsg_ref, gda_ref, sw_ref, sb_ref,
                  psg_ref, pda_ref, wo_ref, h_ref):
    tm = x_ref.shape[0]
    lane = lax.broadcasted_iota(jnp.int32, (CHUNK, LANES), 1)
    first_group = lane < SG_GROUP_DIM
    bias = sb_ref[...]
    rows = []
    for c in range(tm // CHUNK):
        vc = vn_ref[c * CHUNK:(c + 1) * CHUNK, :]
        pairs = []
        for pr in range(SG_GROUPS // 2):
            vp = vc[:, pr * LANES:(pr + 1) * LANES]
            pairs.append(jnp.where(first_group, _dot(sw_ref[2 * pr], vp), _dot(sw_ref[2 * pr + 1], vp)))
        sv = jnp.concatenate(pairs, axis=1) + bias
        rows.append((gu_ref[c * CHUNK:(c + 1) * CHUNK, :].astype(F32) * sv).astype(BF16))
    o_sg = jnp.concatenate(rows, axis=0)
    y_sg = _dot(o_sg, psg_ref[...])
    y_da = _dot(oda_ref[...], pda_ref[...])
    z = (gsg_ref[...].astype(F32) * y_sg + gda_ref[...].astype(F32) * y_da).astype(BF16)
    h_ref[...] = x_ref[...] + _dot(z, wo_ref[...])


def _merge(x, gu, vn, o_da, g_sg, g_da, sg_w, sg_bias, p_sg, p_da, w_out):
    B, S, D = x.shape
    tm = TM_MERGE
    tok = lambda width: pl.BlockSpec((None, tm, width), lambda b, i: (b, i, 0))
    return pl.pallas_call(
        _merge_kernel,
        grid=(B, S // tm),
        in_specs=[
            tok(D), tok(SG_WIDTH), tok(SG_WIDTH), tok(DA_WIDTH), tok(D_MODEL), tok(D_MODEL),
            _const_spec(sg_w.shape), _const_spec(sg_bias.shape),
            _const_spec(p_sg.shape), _const_spec(p_da.shape), _const_spec(w_out.shape),
        ],
        out_specs=tok(D),
        out_shape=jax.ShapeDtypeStruct((B, S, D), F32),
        compiler_params=pltpu.CompilerParams(vmem_limit_bytes=VMEM_LIMIT_BYTES),
        name="merge",
    )(x, gu, vn, o_da, g_sg, g_da, sg_w, sg_bias, p_sg, p_da, w_out)


def _ffn_kernel(h_ref, g_ref, wg_ref, wu_ref, wd_ref, o_ref):
    hres = h_ref[...]
    ms = jnp.mean(hres * hres, axis=-1, keepdims=True)
    hn = (hres * lax.rsqrt(ms + EPS) * g_ref[...]).astype(BF16)
    d_ff = wg_ref.shape[1]
    acc = hres
    for c in range(d_ff // FF_CHUNK):
        sl = slice(c * FF_CHUNK, (c + 1) * FF_CHUNK)
        a = jax.nn.silu(_dot(hn, wg_ref[:, sl])) * _dot(hn, wu_ref[:, sl])
        acc = acc + _dot(a.astype(BF16), wd_ref[sl, :])
    o_ref[...] = acc


def _ffn(h, g, w_gate, w_up, w_down):
    B, S, D = h.shape
    tm = TM_FFN
    tok = pl.BlockSpec((None, tm, D), lambda b, i: (b, i, 0))
    return pl.pallas_call(
        _ffn_kernel,
        grid=(B, S // tm),
        in_specs=[tok, _const_spec((1, D)), _const_spec(w_gate.shape), _const_spec(w_up.shape),
                  _const_spec(w_down.shape)],
        out_specs=tok,
        out_shape=jax.ShapeDtypeStruct((B, S, D), F32),
        compiler_params=pltpu.CompilerParams(vmem_limit_bytes=VMEM_LIMIT_BYTES),
        name="swiglu_ffn",
    )(h, g, w_gate, w_up, w_down)


def kernel(x, norm1_g, w_in, b_gate, sg_ln_g, sg_ln_b, sg_w, sg_b, q_norm_g, k_norm_g, lam_q1, lam_k1, lam_q2, lam_k2, subln_g, w_proj_sg, w_proj_da, w_out, norm2_g, w_ffn_gate, w_ffn_up, w_ffn_down):
    depth = w_in.shape[0]
    row = lambda a: a.reshape(1, -1).astype(F32)
    n_rep = DA_WIDTH // DA_HEAD_DIM
    slopes = jnp.asarray([2.0 ** (-8.0 * (i + 1) / DA_HEADS) * LOG2E for i in range(DA_HEADS)], F32)
    for l in range(depth):
        qg = row(jnp.tile(q_norm_g[l], n_rep)) * (LOG2E / math.sqrt(DA_HEAD_DIM))
        kg = row(jnp.tile(k_norm_g[l], n_rep))
        gu, vn, q, kt, v, g_sg, g_da = _in_projection(
            x, row(norm1_g[l]), w_in[l].astype(BF16), row(b_gate[l]),
            row(sg_ln_g[l]), row(sg_ln_b[l]), qg, kg)
        o_da = _attention(slopes, q, kt, v, row(lam_q1[l]), row(lam_k1[l]), row(lam_q2[l]),
                          row(lam_k2[l]), row(subln_g[l]))
        sg_bias = jnp.repeat(sg_b[l].T.astype(F32), SG_GROUP_DIM, axis=1)
        h = _merge(x, gu, vn, o_da, g_sg, g_da, sg_w[l].astype(BF16), sg_bias,
                   w_proj_sg[l].astype(BF16), w_proj_da[l].astype(BF16), w_out[l].astype(BF16))
        x = _ffn(h, row(norm2_g[l]), w_ffn_gate[l].astype(BF16), w_ffn_up[l].astype(BF16),
                 w_ffn_down[l].astype(BF16))
    return x
```

```python
import math

import jax
import jax.numpy as jnp
from jax import lax
from jax.experimental import pallas as pl
from jax.experimental.pallas import tpu as pltpu

D_MODEL = 1024
SG_GROUPS = 8
SG_GROUP_DIM = 64
SG_WIDTH = SG_GROUPS * SG_GROUP_DIM
CHUNK = 128
DA_HEADS = 8
DA_HEAD_DIM = 64
DA_V_DIM = 2 * DA_HEAD_DIM
DA_WIDTH = DA_HEADS * DA_V_DIM
EPS = 1e-6
LAMBDA_INIT = 0.8 - 0.6 * math.exp(-0.3 * 0)
LOG2E = math.log2(math.e)

BF16 = jnp.bfloat16
F32 = jnp.float32

LANES = 128
BF16_SUBLANES = 16
MXU_DIM = 256
VMEM_LIMIT_BYTES = 56 * 1024 * 1024

TM_PROJ = 512
TQ = 512
TK = 512
POS_SPLIT = 64
N_AUG = 15
SAFE_EXP2_RANGE = 60.0
KMAX_MARGIN = 1.01
assert TQ == TK and N_AUG <= BF16_SUBLANES
TM_MERGE = 512
TM_FFN = 512
FF_CHUNK = 256


def _dot(a, b):
    return jnp.dot(a, b, preferred_element_type=F32)


def _const_spec(shape):
    n = len(shape)
    return pl.BlockSpec(shape, lambda *_: (0,) * n, pipeline_mode=pl.Buffered(1))


def _group_ones(n):
    r = lax.broadcasted_iota(jnp.int32, (n, n), 0) // DA_HEAD_DIM
    c = lax.broadcasted_iota(jnp.int32, (n, n), 1) // DA_HEAD_DIM
    return jnp.where(r == c, 1.0, 0.0).astype(BF16)


def _group_mean_sq(p):
    ones_bd = _group_ones(MXU_DIM)
    sq = (p * p).astype(BF16)
    parts = [_dot(sq[:, i * MXU_DIM:(i + 1) * MXU_DIM], ones_bd) for i in range(p.shape[1] // MXU_DIM)]
    return jnp.concatenate(parts, axis=1) * (1.0 / DA_HEAD_DIM)


def _inproj_kernel(x_ref, n1g_ref, w_ref, bg_ref, lng_ref, lnb_ref, qg_ref, kg_ref,
                   gu_ref, vn_ref, qt_ref, k_ref, vt_ref, gsg_ref, gda_ref):
    x = x_ref[...]
    ms = jnp.mean(x * x, axis=-1, keepdims=True)
    xn = (x * lax.rsqrt(ms + EPS) * n1g_ref[...]).astype(BF16)

    p = _dot(xn, w_ref[:, 0:2 * SG_WIDTH])
    gu_ref[...] = jax.nn.gelu(p[:, :SG_WIDTH]).astype(BF16)
    gv = jax.nn.gelu(p[:, SG_WIDTH:])
    mu = jnp.mean(gv, axis=-1, keepdims=True)
    cen = gv - mu
    var = jnp.mean(cen * cen, axis=-1, keepdims=True)
    vn_ref[...] = (cen * lax.rsqrt(var + EPS) * lng_ref[...] + lnb_ref[...]).astype(BF16)

    c0 = 2 * SG_WIDTH
    p = _dot(xn, w_ref[:, c0:c0 + DA_WIDTH])
    qn = p * lax.rsqrt(_group_mean_sq(p) + EPS) * qg_ref[...]
    for h in range(DA_HEADS):
        qt_ref[h] = qn[:, h * DA_V_DIM:(h + 1) * DA_V_DIM].T.astype(BF16)

    c0 += DA_WIDTH
    p = _dot(xn, w_ref[:, c0:c0 + DA_WIDTH])
    kn = (p * lax.rsqrt(_group_mean_sq(p) + EPS) * kg_ref[...]).astype(BF16)
    for h in range(DA_HEADS):
        k_ref[h] = kn[:, h * DA_V_DIM:(h + 1) * DA_V_DIM]

    c0 += DA_WIDTH
    p = _dot(xn, w_ref[:, c0:c0 + DA_WIDTH])
    for h in range(DA_HEADS):
        vt_ref[h] = p[:, h * DA_V_DIM:(h + 1) * DA_V_DIM].T.astype(BF16)

    c0 += DA_WIDTH
    p = _dot(xn, w_ref[:, c0:c0 + D_MODEL])
    gsg_ref[...] = jax.nn.sigmoid(p + bg_ref[:, :D_MODEL]).astype(BF16)
    c0 += D_MODEL
    p = _dot(xn, w_ref[:, c0:c0 + D_MODEL])
    gda_ref[...] = jax.nn.sigmoid(p + bg_ref[:, D_MODEL:]).astype(BF16)


def _in_projection(x, n1g, w_in, b_gate, ln_g, ln_b, qg, kg):
    B, S, D = x.shape
    tm = TM_PROJ
    n_t = S // tm
    n_kc = S // TK
    sub = TK // tm
    in_cols = w_in.shape[1]
    tok = lambda width: pl.BlockSpec((None, tm, width), lambda b, i: (b, i, 0))
    out_shape = (
        jax.ShapeDtypeStruct((B, S, SG_WIDTH), BF16),
        jax.ShapeDtypeStruct((B, S, SG_WIDTH), BF16),
        jax.ShapeDtypeStruct((B, DA_HEADS, DA_V_DIM, S), BF16),
        jax.ShapeDtypeStruct((B, DA_HEADS, S, DA_V_DIM), BF16),
        jax.ShapeDtypeStruct((B, DA_HEADS, n_kc, DA_V_DIM, TK), BF16),
        jax.ShapeDtypeStruct((B, S, D_MODEL), BF16),
        jax.ShapeDtypeStruct((B, S, D_MODEL), BF16),
    )
    out_specs = (
        tok(SG_WIDTH), tok(SG_WIDTH),
        pl.BlockSpec((None, DA_HEADS, DA_V_DIM, tm), lambda b, i: (b, 0, 0, i)),
        pl.BlockSpec((None, DA_HEADS, tm, DA_V_DIM), lambda b, i: (b, 0, i, 0)),
        pl.BlockSpec((None, DA_HEADS, None, DA_V_DIM, tm), lambda b, i: (b, 0, i // sub, 0, i % sub)),
        tok(D_MODEL), tok(D_MODEL),
    )
    in_specs = [
        tok(D),
        _const_spec((1, D)),
        _const_spec((D, in_cols)),
        _const_spec((1, 2 * D_MODEL)),
        _const_spec((1, SG_WIDTH)),
        _const_spec((1, SG_WIDTH)),
        _const_spec((1, DA_WIDTH)),
        _const_spec((1, DA_WIDTH)),
    ]
    return pl.pallas_call(
        _inproj_kernel,
        grid=(B, n_t),
        in_specs=in_specs,
        out_specs=out_specs,
        out_shape=out_shape,
        compiler_params=pltpu.CompilerParams(vmem_limit_bytes=VMEM_LIMIT_BYTES),
        name="in_projection",
    )(x, n1g, w_in, b_gate, ln_g, ln_b, qg, kg)


def _split3(x):
    hi = x.astype(BF16).astype(F32)
    r = x - hi
    mid = r.astype(BF16).astype(F32)
    lo = (r - mid).astype(BF16).astype(F32)
    return hi, mid, lo


def _alibi_constants(slopes, seq, tq):
    parts = jnp.stack(_split3(slopes), axis=0)
    idx = jnp.arange(LANES, dtype=jnp.int32)
    part_of = parts[idx % 3].T
    kpos = jnp.arange(seq, dtype=jnp.int32)[None, :, None]
    a = idx[None, None, :]
    kcol = jnp.where(a < 3, -1.0, jnp.where(a < 9, -part_of[:, None, :],
           jnp.where(a < 12, ((kpos // POS_SPLIT) * POS_SPLIT).astype(F32),
           jnp.where(a < N_AUG, (kpos % POS_SPLIT).astype(F32), 0.0))))
    qloc = (jnp.arange(2 * tq, dtype=jnp.int32) % tq)[None, None, :]
    a = jnp.arange(BF16_SUBLANES, dtype=jnp.int32)[None, :, None]
    qrow = jnp.where((a >= 3) & (a < 6), ((qloc // POS_SPLIT) * POS_SPLIT).astype(F32),
           jnp.where((a >= 6) & (a < 9), (qloc % POS_SPLIT).astype(F32),
           jnp.where((a >= 9) & (a < N_AUG), part_of[:, :BF16_SUBLANES, None], 0.0)))
    return kcol.astype(BF16), qrow


def _attn_kernel(slopes_ref, qt_ref, k_ref, kcol_ref, qrow_ref, dist_ref, vt_ref,
                 lq1_ref, lk1_ref, lq2_ref, lk2_ref, sg_ref, o_ref,
                 qmain_ref, qaug_ref, acc_ref, kmax_ref):
    h = pl.program_id(1)
    i = pl.program_id(2)
    tq = qt_ref.shape[1]
    n_kc, dv, tk = vt_ref.shape
    slope = slopes_ref[h]

    def k_chunk(j):
        return k_ref[pl.ds(pl.multiple_of(j * tk, tk), tk), :]

    @pl.when(i == 0)
    def _():
        qaug_ref[:, BF16_SUBLANES:, :] = jnp.zeros((3, LANES - BF16_SUBLANES, 2 * tq), BF16)
        ones_bd = _group_ones(LANES)
        best = jnp.zeros((1, LANES), F32)
        for c in range(n_kc):
            kf = k_ref[c * tk:(c + 1) * tk, :].astype(F32)
            best = jnp.maximum(best, jnp.max(_dot((kf * kf).astype(BF16), ones_bd), axis=0, keepdims=True))
        kmax_ref[0:1, :] = jnp.sqrt(best) * KMAX_MARGIN

    qt = qt_ref[...]
    sub = lax.broadcasted_iota(jnp.int32, (dv, tq), 0)
    zero = jnp.zeros_like(qt)
    qmain_ref[:, 0:tq] = jnp.where(sub < DA_HEAD_DIM, qt, zero)
    qmain_ref[:, tq:2 * tq] = jnp.where(sub < DA_HEAD_DIM, zero, qt)
    qsq = qt.astype(F32) * qt.astype(F32)
    n1 = jnp.sqrt(jnp.sum(qsq[:DA_HEAD_DIM], axis=0, keepdims=True))
    n2 = jnp.sqrt(jnp.sum(qsq[DA_HEAD_DIM:], axis=0, keepdims=True))
    m = jnp.concatenate([n1 * kmax_ref[0:1, 0:1], n2 * kmax_ref[0:1, DA_HEAD_DIM:DA_HEAD_DIM + 1]], axis=1)

    a = lax.broadcasted_iota(jnp.int32, (BF16_SUBLANES, 2 * tq), 0)
    positional = qrow_ref[...] + jnp.where((a >= 3) & (a < 6), (i * tq).astype(F32), 0.0)
    m_hi, m_mid, m_lo = _split3(m)
    m_rows = jnp.where(a == 0, m_hi, jnp.where(a == 1, m_mid, m_lo))
    for variant, tail in enumerate((positional, -positional, jnp.zeros_like(positional))):
        qaug_ref[variant, 0:BF16_SUBLANES, :] = jnp.where(a < 3, m_rows, tail).astype(BF16)

    fast = jnp.max(m) <= SAFE_EXP2_RANGE

    @pl.when(fast)
    def _():
        ones_row = jnp.where(lax.broadcasted_iota(jnp.int32, (BF16_SUBLANES, tk), 0) == 0, 1.0, 0.0).astype(BF16)

        def chunk(j, variant, bias):
            k_aug = jnp.concatenate([k_chunk(j), kcol_ref[pl.ds(pl.multiple_of(j * tk, tk), tk), :]], axis=1)
            e = _dot(k_aug, jnp.concatenate([qmain_ref[...], qaug_ref[variant]], axis=0))
            if bias is not None:
                e = e - bias
            p = jnp.exp2(e).astype(BF16)
            return _dot(jnp.concatenate([vt_ref[j], ones_row], axis=0), p)

        diag_bias = dist_ref[...] * slope
        acc = chunk(i, 2, jnp.concatenate([diag_bias, diag_bias], axis=1))
        for t in range(n_kc - 1):
            j = t + (t >= i).astype(jnp.int32)
            acc = acc + chunk(j, (j > i).astype(jnp.int32), None)
        acc_ref[...] = acc

    @pl.when(jnp.logical_not(fast))
    def _():
        signed = (lax.broadcasted_iota(jnp.int32, (tk, 2 * tq), 1) % tq
                  - lax.broadcasted_iota(jnp.int32, (tk, 2 * tq), 0)).astype(F32) * slope

        def body(j, carry):
            mx, l, acc = carry
            off = (i * tq - j * tk).astype(F32) * slope
            s = _dot(k_chunk(j), qmain_ref[...]) - jnp.abs(signed + off)
            m_new = jnp.maximum(mx, jnp.max(s, axis=0, keepdims=True))
            alpha = jnp.exp2(mx - m_new)
            p = jnp.exp2(s - m_new)
            l = alpha * l + jnp.sum(p, axis=0, keepdims=True)
            acc = alpha * acc + _dot(vt_ref[j], p.astype(BF16))
            return m_new, l, acc

        m0 = jnp.full((1, 2 * tq), -jnp.inf, F32)
        l0 = jnp.zeros((1, 2 * tq), F32)
        a0 = jnp.zeros((dv, 2 * tq), F32)
        _, l, acc = lax.fori_loop(0, n_kc, body, (m0, l0, a0))
        acc_ref[0:dv, :] = acc
        acc_ref[dv:dv + BF16_SUBLANES, :] = jnp.broadcast_to(l, (BF16_SUBLANES, 2 * tq))

    lam = (jnp.exp(jnp.sum(lq1_ref[...] * lk1_ref[...], axis=-1, keepdims=True))
           - jnp.exp(jnp.sum(lq2_ref[...] * lk2_ref[...], axis=-1, keepdims=True)) + LAMBDA_INIT)
    o = acc_ref[0:dv, :] / acc_ref[dv:dv + 1, :]
    o = o[:, :tq] - lam * o[:, tq:]
    ms = jnp.mean(o * o, axis=0, keepdims=True)
    o = (o * lax.rsqrt(ms + EPS)).T
    o_ref[...] = (o * sg_ref[...] * (1.0 - LAMBDA_INIT)).astype(BF16)


def _attention(slopes, qt, k, vt, lq1, lk1, lq2, lk2, subln_g):
    B, H, dv, S = qt.shape
    n_kc = vt.shape[2]
    kcol, qrow = _alibi_constants(slopes, S, TQ)
    dist = jnp.abs(jnp.arange(TK, dtype=jnp.int32)[:, None]
                   - jnp.arange(TQ, dtype=jnp.int32)[None, :]).astype(F32)
    small = lambda n: pl.BlockSpec((1, n), lambda b, h, i, *_: (0, 0))
    grid_spec = pltpu.PrefetchScalarGridSpec(
        num_scalar_prefetch=1,
        grid=(B, H, S // TQ),
        in_specs=[
            pl.BlockSpec((None, None, dv, TQ), lambda b, h, i, *_: (b, h, 0, i)),
            pl.BlockSpec((None, None, S, dv), lambda b, h, i, *_: (b, h, 0, 0)),
            pl.BlockSpec((None, S, LANES), lambda b, h, i, *_: (h, 0, 0)),
            pl.BlockSpec((None, BF16_SUBLANES, 2 * TQ), lambda b, h, i, *_: (h, 0, 0)),
            pl.BlockSpec((TK, TQ), lambda b, h, i, *_: (0, 0)),
            pl.BlockSpec((None, None, n_kc, dv, TK), lambda b, h, i, *_: (b, h, 0, 0, 0)),
            small(DA_HEAD_DIM), small(DA_HEAD_DIM), small(DA_HEAD_DIM), small(DA_HEAD_DIM),
            small(DA_V_DIM),
        ],
        out_specs=pl.BlockSpec((None, TQ, dv), lambda b, h, i, *_: (b, i, h)),
        scratch_shapes=[
            pltpu.VMEM((dv, 2 * TQ), BF16),
            pltpu.VMEM((3, LANES, 2 * TQ), BF16),
            pltpu.VMEM((dv + BF16_SUBLANES, 2 * TQ), F32),
            pltpu.VMEM((8, LANES), F32),
        ],
    )
    return pl.pallas_call(
        _attn_kernel,
        grid_spec=grid_spec,
        out_shape=jax.ShapeDtypeStruct((B, S, H * dv), BF16),
        compiler_params=pltpu.CompilerParams(vmem_limit_bytes=VMEM_LIMIT_BYTES),
        name="diff_attention",
    )(slopes, qt, k, kcol, qrow, dist, vt, lq1, lk1, lq2, lk2, subln_g)


def _merge_kernel(x_ref, gu_ref, vn_ref, oda_ref, gsg_ref, gda_ref, sw_ref, sb_ref,
                  psg_ref, pda_ref, wo_ref, h_ref):
    tm = x_ref.shape[0]
    lane = lax.broadcasted_iota(jnp.int32, (CHUNK, LANES), 1)
    first_group = lane < SG_GROUP_DIM
    bias = sb_ref[...]
    rows = []
    for c in range(tm // CHUNK):
        vc = vn_ref[c * CHUNK:(c + 1) * CHUNK, :]
        pairs = []
        for pr in range(SG_GROUPS // 2):
            vp = vc[:, pr * LANES:(pr + 1) * LANES]
            pairs.append(jnp.where(first_group, _dot(sw_ref[2 * pr], vp), _dot(sw_ref[2 * pr + 1], vp)))
        sv = jnp.concatenate(pairs, axis=1) + bias
        rows.append((gu_ref[c * CHUNK:(c + 1) * CHUNK, :].astype(F32) * sv).astype(BF16))
    o_sg = jnp.concatenate(rows, axis=0)
    y_sg = _dot(o_sg, psg_ref[...])
    y_da = _dot(oda_ref[...], pda_ref[...])
    z = (gsg_ref[...].astype(F32) * y_sg + gda_ref[...].astype(F32) * y_da).astype(BF16)
    h_ref[...] = x_ref[...] + _dot(z, wo_ref[...])


def _merge(x, gu, vn, o_da, g_sg, g_da, sg_w, sg_bias, p_sg, p_da, w_out):
    B, S, D = x.shape
    tm = TM_MERGE
    tok = lambda width: pl.BlockSpec((None, tm, width), lambda b, i: (b, i, 0))
    return pl.pallas_call(
        _merge_kernel,
        grid=(B, S // tm),
        in_specs=[
            tok(D), tok(SG_WIDTH), tok(SG_WIDTH), tok(DA_WIDTH), tok(D_MODEL), tok(D_MODEL),
            _const_spec(sg_w.shape), _const_spec(sg_bias.shape),
            _const_spec(p_sg.shape), _const_spec(p_da.shape), _const_spec(w_out.shape),
        ],
        out_specs=tok(D),
        out_shape=jax.ShapeDtypeStruct((B, S, D), F32),
        compiler_params=pltpu.CompilerParams(vmem_limit_bytes=VMEM_LIMIT_BYTES),
        name="merge",
    )(x, gu, vn, o_da, g_sg, g_da, sg_w, sg_bias, p_sg, p_da, w_out)


def _ffn_kernel(h_ref, g_ref, wg_ref, wu_ref, wd_ref, o_ref):
    hres = h_ref[...]
    ms = jnp.mean(hres * hres, axis=-1, keepdims=True)
    hn = (hres * lax.rsqrt(ms + EPS) * g_ref[...]).astype(BF16)
    d_ff = wg_ref.shape[1]
    acc = hres
    for c in range(d_ff // FF_CHUNK):
        sl = slice(c * FF_CHUNK, (c + 1) * FF_CHUNK)
        a = jax.nn.silu(_dot(hn, wg_ref[:, sl])) * _dot(hn, wu_ref[:, sl])
        acc = acc + _dot(a.astype(BF16), wd_ref[sl, :])
    o_ref[...] = acc


def _ffn(h, g, w_gate, w_up, w_down):
    B, S, D = h.shape
    tm = TM_FFN
    tok = pl.BlockSpec((None, tm, D), lambda b, i: (b, i, 0))
    return pl.pallas_call(
        _ffn_kernel,
        grid=(B, S // tm),
        in_specs=[tok, _const_spec((1, D)), _const_spec(w_gate.shape), _const_spec(w_up.shape),
                  _const_spec(w_down.shape)],
        out_specs=tok,
        out_shape=jax.ShapeDtypeStruct((B, S, D), F32),
        compiler_params=pltpu.CompilerParams(vmem_limit_bytes=VMEM_LIMIT_BYTES),
        name="swiglu_ffn",
    )(h, g, w_gate, w_up, w_down)


def kernel(x, norm1_g, w_in, b_gate, sg_ln_g, sg_ln_b, sg_w, sg_b, q_norm_g, k_norm_g, lam_q1, lam_k1, lam_q2, lam_k2, subln_g, w_proj_sg, w_proj_da, w_out, norm2_g, w_ffn_gate, w_ffn_up, w_ffn_down):
    depth = w_in.shape[0]
    row = lambda a: a.reshape(1, -1).astype(F32)
    n_rep = DA_WIDTH // DA_HEAD_DIM
    slopes = jnp.asarray([2.0 ** (-8.0 * (i + 1) / DA_HEADS) * LOG2E for i in range(DA_HEADS)], F32)
    for l in range(depth):
        qg = row(jnp.tile(q_norm_g[l], n_rep)) * (LOG2E / math.sqrt(DA_HEAD_DIM))
        kg = row(jnp.tile(k_norm_g[l], n_rep))
        gu, vn, qt, k, vt, g_sg, g_da = _in_projection(
            x, row(norm1_g[l]), w_in[l].astype(BF16), row(b_gate[l]),
            row(sg_ln_g[l]), row(sg_ln_b[l]), qg, kg)
        o_da = _attention(slopes, qt, k, vt, row(lam_q1[l]), row(lam_k1[l]), row(lam_q2[l]),
                          row(lam_k2[l]), row(subln_g[l]))
        sg_bias = jnp.repeat(sg_b[l].T.astype(F32), SG_GROUP_DIM, axis=1)
        h = _merge(x, gu, vn, o_da, g_sg, g_da, sg_w[l].astype(BF16), sg_bias,
                   w_proj_sg[l].astype(BF16), w_proj_da[l].astype(BF16), w_out[l].astype(BF16))
        x = _ffn(h, row(norm2_g[l]), w_ffn_gate[l].astype(BF16), w_ffn_up[l].astype(BF16),
                 w_ffn_down[l].astype(BF16))
    return x
```

```python
import math

import jax
import jax.numpy as jnp
from jax import lax
from jax.experimental import pallas as pl
from jax.experimental.pallas import tpu as pltpu

D_MODEL = 1024
SG_GROUPS = 8
SG_GROUP_DIM = 64
SG_WIDTH = SG_GROUPS * SG_GROUP_DIM
CHUNK = 128
DA_HEADS = 8
DA_HEAD_DIM = 64
DA_V_DIM = 2 * DA_HEAD_DIM
DA_WIDTH = DA_HEADS * DA_V_DIM
EPS = 1e-6
LAMBDA_INIT = 0.8 - 0.6 * math.exp(-0.3 * 0)
LOG2E = math.log2(math.e)

BF16 = jnp.bfloat16
F32 = jnp.float32

LANES = 128
BF16_SUBLANES = 16
MXU_DIM = 256
VMEM_LIMIT_BYTES = 56 * 1024 * 1024

TM_PROJ = 512
TQ = 512
TK = 512
POS_SPLIT = 64
N_AUG = 15
SAFE_EXP2_RANGE = 60.0
KMAX_MARGIN = 1.01
assert TQ == TK and N_AUG <= BF16_SUBLANES
TM_MERGE = 512
TM_FFN = 512
FF_CHUNK = 256


def _dot(a, b):
    return jnp.dot(a, b, preferred_element_type=F32)


def _const_spec(shape):
    n = len(shape)
    return pl.BlockSpec(shape, lambda *_: (0,) * n, pipeline_mode=pl.Buffered(1))


def _group_ones(n):
    r = lax.broadcasted_iota(jnp.int32, (n, n), 0) // DA_HEAD_DIM
    c = lax.broadcasted_iota(jnp.int32, (n, n), 1) // DA_HEAD_DIM
    return jnp.where(r == c, 1.0, 0.0).astype(BF16)


def _group_mean_sq(p):
    ones_bd = _group_ones(MXU_DIM)
    sq = (p * p).astype(BF16)
    parts = [_dot(sq[:, i * MXU_DIM:(i + 1) * MXU_DIM], ones_bd) for i in range(p.shape[1] // MXU_DIM)]
    return jnp.concatenate(parts, axis=1) * (1.0 / DA_HEAD_DIM)


def _inproj_kernel(x_ref, n1g_ref, w_ref, bg_ref, lng_ref, lnb_ref, qg_ref, kg_ref,
                   gu_ref, vn_ref, qt_ref, k_ref, vt_ref, gsg_ref, gda_ref):
    x = x_ref[...]
    ms = jnp.mean(x * x, axis=-1, keepdims=True)
    xn = (x * lax.rsqrt(ms + EPS) * n1g_ref[...]).astype(BF16)

    p = _dot(xn, w_ref[:, 0:2 * SG_WIDTH])
    gu_ref[...] = jax.nn.gelu(p[:, :SG_WIDTH]).astype(BF16)
    gv = jax.nn.gelu(p[:, SG_WIDTH:])
    mu = jnp.mean(gv, axis=-1, keepdims=True)
    cen = gv - mu
    var = jnp.mean(cen * cen, axis=-1, keepdims=True)
    vn_ref[...] = (cen * lax.rsqrt(var + EPS) * lng_ref[...] + lnb_ref[...]).astype(BF16)

    c0 = 2 * SG_WIDTH
    p = _dot(xn, w_ref[:, c0:c0 + DA_WIDTH])
    qn = p * lax.rsqrt(_group_mean_sq(p) + EPS) * qg_ref[...]
    for h in range(DA_HEADS):
        qt_ref[h] = qn[:, h * DA_V_DIM:(h + 1) * DA_V_DIM].T.astype(BF16)

    c0 += DA_WIDTH
    p = _dot(xn, w_ref[:, c0:c0 + DA_WIDTH])
    kn = (p * lax.rsqrt(_group_mean_sq(p) + EPS) * kg_ref[...]).astype(BF16)
    for h in range(DA_HEADS):
        k_ref[h] = kn[:, h * DA_V_DIM:(h + 1) * DA_V_DIM]

    c0 += DA_WIDTH
    p = _dot(xn, w_ref[:, c0:c0 + DA_WIDTH])
    for h in range(DA_HEADS):
        vt_ref[h] = p[:, h * DA_V_DIM:(h + 1) * DA_V_DIM].T.astype(BF16)

    c0 += DA_WIDTH
    p = _dot(xn, w_ref[:, c0:c0 + D_MODEL])
    gsg_ref[...] = jax.nn.sigmoid(p + bg_ref[:, :D_MODEL]).astype(BF16)
    c0 += D_MODEL
    p = _dot(xn, w_ref[:, c0:c0 + D_MODEL])
    gda_ref[...] = jax.nn.sigmoid(p + bg_ref[:, D_MODEL:]).astype(BF16)


def _in_projection(x, n1g, w_in, b_gate, ln_g, ln_b, qg, kg):
    B, S, D = x.shape
    tm = TM_PROJ
    n_t = S // tm
    n_kc = S // TK
    sub = TK // tm
    in_cols = w_in.shape[1]
    tok = lambda width: pl.BlockSpec((None, tm, width), lambda b, i: (b, i, 0))
    out_shape = (
        jax.ShapeDtypeStruct((B, S, SG_WIDTH), BF16),
        jax.ShapeDtypeStruct((B, S, SG_WIDTH), BF16),
        jax.ShapeDtypeStruct((B, DA_HEADS, S // TQ, DA_V_DIM, TQ), BF16),
        jax.ShapeDtypeStruct((B, DA_HEADS, S, DA_V_DIM), BF16),
        jax.ShapeDtypeStruct((B, DA_HEADS, n_kc, DA_V_DIM, TK), BF16),
        jax.ShapeDtypeStruct((B, S, D_MODEL), BF16),
        jax.ShapeDtypeStruct((B, S, D_MODEL), BF16),
    )
    out_specs = (
        tok(SG_WIDTH), tok(SG_WIDTH),
        pl.BlockSpec((None, DA_HEADS, None, DA_V_DIM, tm), lambda b, i: (b, 0, i // sub, 0, i % sub)),
        pl.BlockSpec((None, DA_HEADS, tm, DA_V_DIM), lambda b, i: (b, 0, i, 0)),
        pl.BlockSpec((None, DA_HEADS, None, DA_V_DIM, tm), lambda b, i: (b, 0, i // sub, 0, i % sub)),
        tok(D_MODEL), tok(D_MODEL),
    )
    in_specs = [
        tok(D),
        _const_spec((1, D)),
        _const_spec((D, in_cols)),
        _const_spec((1, 2 * D_MODEL)),
        _const_spec((1, SG_WIDTH)),
        _const_spec((1, SG_WIDTH)),
        _const_spec((1, DA_WIDTH)),
        _const_spec((1, DA_WIDTH)),
    ]
    return pl.pallas_call(
        _inproj_kernel,
        grid=(B, n_t),
        in_specs=in_specs,
        out_specs=out_specs,
        out_shape=out_shape,
        compiler_params=pltpu.CompilerParams(vmem_limit_bytes=VMEM_LIMIT_BYTES),
        name="in_projection",
    )(x, n1g, w_in, b_gate, ln_g, ln_b, qg, kg)


def _split3(x):
    hi = x.astype(BF16).astype(F32)
    r = x - hi
    mid = r.astype(BF16).astype(F32)
    lo = (r - mid).astype(BF16).astype(F32)
    return hi, mid, lo


def _alibi_constants(slopes, seq, tq):
    parts = jnp.stack(_split3(slopes), axis=0)
    idx = jnp.arange(LANES, dtype=jnp.int32)
    part_of = parts[idx % 3].T
    kpos = jnp.arange(seq, dtype=jnp.int32)[None, :, None]
    a = idx[None, None, :]
    kcol = jnp.where(a < 3, -1.0, jnp.where(a < 9, -part_of[:, None, :],
           jnp.where(a < 12, ((kpos // POS_SPLIT) * POS_SPLIT).astype(F32),
           jnp.where(a < N_AUG, (kpos % POS_SPLIT).astype(F32), 0.0))))
    qloc = (jnp.arange(2 * tq, dtype=jnp.int32) % tq)[None, None, :]
    a = jnp.arange(BF16_SUBLANES, dtype=jnp.int32)[None, :, None]
    qrow = jnp.where((a >= 3) & (a < 6), ((qloc // POS_SPLIT) * POS_SPLIT).astype(F32),
           jnp.where((a >= 6) & (a < 9), (qloc % POS_SPLIT).astype(F32),
           jnp.where((a >= 9) & (a < N_AUG), part_of[:, :BF16_SUBLANES, None], 0.0)))
    return kcol.astype(BF16), qrow


def _attn_kernel(slopes_ref, qt_ref, k_ref, kcol_ref, qrow_ref, dist_ref, vt_ref,
                 lq1_ref, lk1_ref, lq2_ref, lk2_ref, sg_ref, o_ref,
                 qmain_ref, qaug_ref, acc_ref, kmax_ref, fast_ref):
    h = pl.program_id(1)
    n_q, dv, tq = qt_ref.shape
    n_kc, _, tk = vt_ref.shape
    slope = slopes_ref[h]

    def rows_of(index, size):
        start = index * size
        return pl.ds(start if isinstance(start, int) else pl.multiple_of(start, size), size)

    def k_chunk(j):
        return k_ref[rows_of(j, tk), :]

    qaug_ref[:, :, BF16_SUBLANES:, :] = jnp.zeros((2, 3, LANES - BF16_SUBLANES, 2 * tq), BF16)
    ones_bd = _group_ones(LANES)
    best = jnp.zeros((1, LANES), F32)
    for c in range(n_kc):
        kf = k_ref[c * tk:(c + 1) * tk, :].astype(F32)
        best = jnp.maximum(best, jnp.max(_dot((kf * kf).astype(BF16), ones_bd), axis=0, keepdims=True))
    kmax_ref[0:1, :] = jnp.sqrt(best) * KMAX_MARGIN
    lam = (jnp.exp(jnp.sum(lq1_ref[...] * lk1_ref[...], axis=-1, keepdims=True))
           - jnp.exp(jnp.sum(lq2_ref[...] * lk2_ref[...], axis=-1, keepdims=True)) + LAMBDA_INIT)

    def prepare(blk, slot):
        qt = qt_ref[blk]
        sub = lax.broadcasted_iota(jnp.int32, (dv, tq), 0)
        zero = jnp.zeros_like(qt)
        qmain_ref[slot, :, 0:tq] = jnp.where(sub < DA_HEAD_DIM, qt, zero)
        qmain_ref[slot, :, tq:2 * tq] = jnp.where(sub < DA_HEAD_DIM, zero, qt)
        qsq = qt.astype(F32) * qt.astype(F32)
        n1 = jnp.sqrt(jnp.sum(qsq[:DA_HEAD_DIM], axis=0, keepdims=True))
        n2 = jnp.sqrt(jnp.sum(qsq[DA_HEAD_DIM:], axis=0, keepdims=True))
        m = jnp.concatenate([n1 * kmax_ref[0:1, 0:1], n2 * kmax_ref[0:1, DA_HEAD_DIM:DA_HEAD_DIM + 1]], axis=1)
        a = lax.broadcasted_iota(jnp.int32, (BF16_SUBLANES, 2 * tq), 0)
        positional = qrow_ref[...] + jnp.where((a >= 3) & (a < 6), jnp.asarray(blk * tq, F32), 0.0)
        m_hi, m_mid, m_lo = _split3(m)
        m_rows = jnp.where(a == 0, m_hi, jnp.where(a == 1, m_mid, m_lo))
        for variant, tail in enumerate((positional, -positional, jnp.zeros_like(positional))):
            qaug_ref[slot, variant, 0:BF16_SUBLANES, :] = jnp.where(a < 3, m_rows, tail).astype(BF16)
        fast_ref[slot] = jnp.where(jnp.max(m) <= SAFE_EXP2_RANGE, 1, 0)

    def finish(blk, slot):
        o = acc_ref[slot, 0:dv, :] / acc_ref[slot, dv:dv + 1, :]
        o = o[:, :tq] - lam * o[:, tq:]
        ms = jnp.mean(o * o, axis=0, keepdims=True)
        o = (o * lax.rsqrt(ms + EPS)).T
        o_ref[rows_of(blk, tq), :] =(o * sg_ref[...] * (1.0 - LAMBDA_INIT)).astype(BF16)

    def scores_fast(blk, slot):
        ones_row = jnp.where(lax.broadcasted_iota(jnp.int32, (BF16_SUBLANES, tk), 0) == 0, 1.0, 0.0).astype(BF16)

        def chunk(j, variant, bias):
            k_aug = jnp.concatenate([k_chunk(j), kcol_ref[rows_of(j, tk), :]], axis=1)
            e = _dot(k_aug, jnp.concatenate([qmain_ref[slot], qaug_ref[slot, variant]], axis=0))
            if bias is not None:
                e = e - bias
            p = jnp.exp2(e).astype(BF16)
            return _dot(jnp.concatenate([vt_ref[j], ones_row], axis=0), p)

        diag_bias = dist_ref[...] * slope
        acc = chunk(blk, 2, jnp.concatenate([diag_bias, diag_bias], axis=1))
        for t in range(n_kc - 1):
            j = t + jnp.where(t >= blk, 1, 0)
            acc = acc + chunk(j, jnp.where(j > blk, 1, 0), None)
        acc_ref[slot] = acc

    def scores_online(blk, slot):
        signed = (lax.broadcasted_iota(jnp.int32, (tk, 2 * tq), 1) % tq
                  - lax.broadcasted_iota(jnp.int32, (tk, 2 * tq), 0)).astype(F32) * slope

        def body(j, carry):
            mx, l, acc = carry
            off = jnp.asarray(blk * tq - j * tk, F32) * slope
            s = _dot(k_chunk(j), qmain_ref[slot]) - jnp.abs(signed + off)
            m_new = jnp.maximum(mx, jnp.max(s, axis=0, keepdims=True))
            alpha = jnp.exp2(mx - m_new)
            p = jnp.exp2(s - m_new)
            l = alpha * l + jnp.sum(p, axis=0, keepdims=True)
            acc = alpha * acc + _dot(vt_ref[j], p.astype(BF16))
            return m_new, l, acc

        m0 = jnp.full((1, 2 * tq), -jnp.inf, F32)
        l0 = jnp.zeros((1, 2 * tq), F32)
        a0 = jnp.zeros((dv, 2 * tq), F32)
        _, l, acc = lax.fori_loop(0, n_kc, body, (m0, l0, a0))
        acc_ref[slot, 0:dv, :] = acc
        acc_ref[slot, dv:dv + BF16_SUBLANES, :] = jnp.broadcast_to(l, (BF16_SUBLANES, 2 * tq))

    def stage(blk, slot):
        other = 1 - slot
        prev = jnp.maximum(blk - 1, 0)
        nxt = jnp.minimum(blk + 1, n_q - 1)
        fast = fast_ref[slot] == 1

        @pl.when(fast)
        def _():
            finish(prev, other)
            scores_fast(blk, slot)
            prepare(nxt, other)

        @pl.when(jnp.logical_not(fast))
        def _():
            finish(prev, other)
            scores_online(blk, slot)
            prepare(nxt, other)

    acc_ref[1] = jnp.ones(acc_ref.shape[1:], F32)
    prepare(0, 0)

    def pair(t, carry):
        stage(2 * t, 0)
        stage(2 * t + 1, 1)
        return carry

    lax.fori_loop(0, n_q // 2, pair, 0)
    finish(n_q - 1, 1)


def _attention(slopes, qt, k, vt, lq1, lk1, lq2, lk2, subln_g):
    B, H, n_q, dv, tq = qt.shape
    n_kc = vt.shape[2]
    S = k.shape[2]
    assert n_q % 2 == 0
    kcol, qrow = _alibi_constants(slopes, S, tq)
    dist = jnp.abs(jnp.arange(TK, dtype=jnp.int32)[:, None]
                   - jnp.arange(tq, dtype=jnp.int32)[None, :]).astype(F32)
    small = lambda n: pl.BlockSpec((1, n), lambda b, h, *_: (0, 0))
    grid_spec = pltpu.PrefetchScalarGridSpec(
        num_scalar_prefetch=1,
        grid=(B, H),
        in_specs=[
            pl.BlockSpec((None, None, n_q, dv, tq), lambda b, h, *_: (b, h, 0, 0, 0)),
            pl.BlockSpec((None, None, S, dv), lambda b, h, *_: (b, h, 0, 0)),
            pl.BlockSpec((None, S, LANES), lambda b, h, *_: (h, 0, 0)),
            pl.BlockSpec((None, BF16_SUBLANES, 2 * tq), lambda b, h, *_: (h, 0, 0)),
            pl.BlockSpec((TK, tq), lambda b, h, *_: (0, 0)),
            pl.BlockSpec((None, None, n_kc, dv, TK), lambda b, h, *_: (b, h, 0, 0, 0)),
            small(DA_HEAD_DIM), small(DA_HEAD_DIM), small(DA_HEAD_DIM), small(DA_HEAD_DIM),
            small(DA_V_DIM),
        ],
        out_specs=pl.BlockSpec((None, S, dv), lambda b, h, *_: (b, 0, h)),
        scratch_shapes=[
            pltpu.VMEM((2, dv, 2 * tq), BF16),
            pltpu.VMEM((2, 3, LANES, 2 * tq), BF16),
            pltpu.VMEM((2, dv + BF16_SUBLANES, 2 * tq), F32),
            pltpu.VMEM((8, LANES), F32),
            pltpu.SMEM((2,), jnp.int32),
        ],
    )
    return pl.pallas_call(
        _attn_kernel,
        grid_spec=grid_spec,
        out_shape=jax.ShapeDtypeStruct((B, S, H * dv), BF16),
        compiler_params=pltpu.CompilerParams(vmem_limit_bytes=VMEM_LIMIT_BYTES),
        name="diff_attention",
    )(slopes, qt, k, kcol, qrow, dist, vt, lq1, lk1, lq2, lk2, subln_g)


def _merge_kernel(x_ref, gu_ref, vn_ref, oda_ref, gsg_ref, gda_ref, sw_ref, sb_ref,
                  psg_ref, pda_ref, wo_ref, h_ref):
    tm = x_ref.shape[0]
    lane = lax.broadcasted_iota(jnp.int32, (CHUNK, LANES), 1)
    first_group = lane < SG_GROUP_DIM
    bias = sb_ref[...]
    rows = []
    for c in range(tm // CHUNK):
        vc = vn_ref[c * CHUNK:(c + 1) * CHUNK, :]
        pairs = []
        for pr in range(SG_GROUPS // 2):
            vp = vc[:, pr * LANES:(pr + 1) * LANES]
            pairs.append(jnp.where(first_group, _dot(sw_ref[2 * pr], vp), _dot(sw_ref[2 * pr + 1], vp)))
        sv = jnp.concatenate(pairs, axis=1) + bias
        rows.append((gu_ref[c * CHUNK:(c + 1) * CHUNK, :].astype(F32) * sv).astype(BF16))
    o_sg = jnp.concatenate(rows, axis=0)
    y_sg = _dot(o_sg, psg_ref[...])
    y_da = _dot(oda_ref[...], pda_ref[...])
    z = (gsg_ref[...].astype(F32) * y_sg + gda_ref[...].astype(F32) * y_da).astype(BF16)
    h_ref[...] = x_ref[...] + _dot(z, wo_ref[...])


def _merge(x, gu, vn, o_da, g_sg, g_da, sg_w, sg_bias, p_sg, p_da, w_out):
    B, S, D = x.shape
    tm = TM_MERGE
    tok = lambda width: pl.BlockSpec((None, tm, width), lambda b, i: (b, i, 0))
    return pl.pallas_call(
        _merge_kernel,
        grid=(B, S // tm),
        in_specs=[
            tok(D), tok(SG_WIDTH), tok(SG_WIDTH), tok(DA_WIDTH), tok(D_MODEL), tok(D_MODEL),
            _const_spec(sg_w.shape), _const_spec(sg_bias.shape),
            _const_spec(p_sg.shape), _const_spec(p_da.shape), _const_spec(w_out.shape),
        ],
        out_specs=tok(D),
        out_shape=jax.ShapeDtypeStruct((B, S, D), F32),
        compiler_params=pltpu.CompilerParams(vmem_limit_bytes=VMEM_LIMIT_BYTES),
        name="merge",
    )(x, gu, vn, o_da, g_sg, g_da, sg_w, sg_bias, p_sg, p_da, w_out)


def _ffn_kernel(h_ref, g_ref, wg_ref, wu_ref, wd_ref, o_ref):
    hres = h_ref[...]
    ms = jnp.mean(hres * hres, axis=-1, keepdims=True)
    hn = (hres * lax.rsqrt(ms + EPS) * g_ref[...]).astype(BF16)
    d_ff = wg_ref.shape[1]
    acc = hres
    for c in range(d_ff // FF_CHUNK):
        sl = slice(c * FF_CHUNK, (c + 1) * FF_CHUNK)
        a = jax.nn.silu(_dot(hn, wg_ref[:, sl])) * _dot(hn, wu_ref[:, sl])
        acc = acc + _dot(a.astype(BF16), wd_ref[sl, :])
    o_ref[...] = acc


def _ffn(h, g, w_gate, w_up, w_down):
    B, S, D = h.shape
    tm = TM_FFN
    tok = pl.BlockSpec((None, tm, D), lambda b, i: (b, i, 0))
    return pl.pallas_call(
        _ffn_kernel,
        grid=(B, S // tm),
        in_specs=[tok, _const_spec((1, D)), _const_spec(w_gate.shape), _const_spec(w_up.shape),
                  _const_spec(w_down.shape)],
        out_specs=tok,
        out_shape=jax.ShapeDtypeStruct((B, S, D), F32),
        compiler_params=pltpu.CompilerParams(vmem_limit_bytes=VMEM_LIMIT_BYTES),
        name="swiglu_ffn",
    )(h, g, w_gate, w_up, w_down)


def kernel(x, norm1_g, w_in, b_gate, sg_ln_g, sg_ln_b, sg_w, sg_b, q_norm_g, k_norm_g, lam_q1, lam_k1, lam_q2, lam_k2, subln_g, w_proj_sg, w_proj_da, w_out, norm2_g, w_ffn_gate, w_ffn_up, w_ffn_down):
    depth = w_in.shape[0]
    row = lambda a: a.reshape(1, -1).astype(F32)
    n_rep = DA_WIDTH // DA_HEAD_DIM
    slopes = jnp.asarray([2.0 ** (-8.0 * (i + 1) / DA_HEADS) * LOG2E for i in range(DA_HEADS)], F32)
    for l in range(depth):
        qg = row(jnp.tile(q_norm_g[l], n_rep)) * (LOG2E / math.sqrt(DA_HEAD_DIM))
        kg = row(jnp.tile(k_norm_g[l], n_rep))
        gu, vn, qt, k, vt, g_sg, g_da = _in_projection(
            x, row(norm1_g[l]), w_in[l].astype(BF16), row(b_gate[l]),
            row(sg_ln_g[l]), row(sg_ln_b[l]), qg, kg)
        o_da = _attention(slopes, qt, k, vt, row(lam_q1[l]), row(lam_k1[l]), row(lam_q2[l]),
                          row(lam_k2[l]), row(subln_g[l]))
        sg_bias = jnp.repeat(sg_b[l].T.astype(F32), SG_GROUP_DIM, axis=1)
        h = _merge(x, gu, vn, o_da, g_sg, g_da, sg_w[l].astype(BF16), sg_bias,
                   w_proj_sg[l].astype(BF16), w_proj_da[l].astype(BF16), w_out[l].astype(BF16))
        x = _ffn(h, row(norm2_g[l]), w_ffn_gate[l].astype(BF16), w_ffn_up[l].astype(BF16),
                 w_ffn_down[l].astype(BF16))
    return x
```

```python
import math

import jax
import jax.numpy as jnp
from jax import lax
from jax.experimental import pallas as pl
from jax.experimental.pallas import tpu as pltpu

D_MODEL = 1024
SG_GROUPS = 8
SG_GROUP_DIM = 64
SG_WIDTH = SG_GROUPS * SG_GROUP_DIM
CHUNK = 128
DA_HEADS = 8
DA_HEAD_DIM = 64
DA_V_DIM = 2 * DA_HEAD_DIM
DA_WIDTH = DA_HEADS * DA_V_DIM
EPS = 1e-6
LAMBDA_INIT = 0.8 - 0.6 * math.exp(-0.3 * 0)
LOG2E = math.log2(math.e)

BF16 = jnp.bfloat16
F32 = jnp.float32

LANES = 128
BF16_SUBLANES = 16
MXU_DIM = 256
VMEM_LIMIT_BYTES = 56 * 1024 * 1024

TM_PROJ = 512
TQ = 512
TK = 512
POS_SPLIT = 64
N_AUG = 15
SAFE_EXP2_RANGE = 60.0
KMAX_MARGIN = 1.01
assert TQ == TK and N_AUG <= BF16_SUBLANES
TM_MERGE = 512
TM_FFN = 1024
FF_CHUNK = 256


def _dot(a, b):
    return jnp.dot(a, b, preferred_element_type=F32)


def _const_spec(shape):
    n = len(shape)
    return pl.BlockSpec(shape, lambda *_: (0,) * n, pipeline_mode=pl.Buffered(1))


def _group_ones(n, value=1.0):
    r = lax.broadcasted_iota(jnp.int32, (n, n), 0) // DA_HEAD_DIM
    c = lax.broadcasted_iota(jnp.int32, (n, n), 1) // DA_HEAD_DIM
    return jnp.where(r == c, value, 0.0).astype(BF16)


def _inproj_kernel(x_ref, n1g_ref, w_ref, qg_ref, kg_ref,
                   uv_ref, qt_ref, k_ref, vt_ref, gate_ref, xn_ref):
    x = x_ref[...]
    ms = jnp.mean(x * x, axis=-1, keepdims=True)
    xn_ref[...] = (x * lax.rsqrt(ms + EPS) * n1g_ref[...]).astype(BF16)
    averager = _group_ones(MXU_DIM, 1.0 / DA_HEAD_DIM)

    def unit_rms(p, gain_ref):
        sq = (p * p).astype(BF16)
        mean_sq = jnp.concatenate([_dot(sq[:, i * MXU_DIM:(i + 1) * MXU_DIM], averager)
                                   for i in range(p.shape[1] // MXU_DIM)], axis=1)
        return p * lax.rsqrt(mean_sq + EPS) * gain_ref[...]

    def spatial(p):
        uv_ref[...] = p.astype(BF16)

    def query(p):
        qn = unit_rms(p, qg_ref)
        for h in range(DA_HEADS):
            qt_ref[h] = qn[:, h * DA_V_DIM:(h + 1) * DA_V_DIM].T.astype(BF16)

    def key(p):
        kn = unit_rms(p, kg_ref).astype(BF16)
        for h in range(DA_HEADS):
            k_ref[h] = kn[:, h * DA_V_DIM:(h + 1) * DA_V_DIM]

    def value(p):
        for h in range(DA_HEADS):
            vt_ref[h] = p[:, h * DA_V_DIM:(h + 1) * DA_V_DIM].T.astype(BF16)

    def gate_sg(p):
        gate_ref[:, :D_MODEL] = p.astype(BF16)

    def gate_da(p):
        gate_ref[:, D_MODEL:] = p.astype(BF16)

    col = 0
    for width, epilogue in ((2 * SG_WIDTH, spatial), (DA_WIDTH, query), (DA_WIDTH, key), (DA_WIDTH, value),
                            (D_MODEL, gate_sg), (D_MODEL, gate_da)):
        epilogue(_dot(xn_ref[...], w_ref[:, col:col + width]))
        col += width


def _in_projection(x, n1g, w_in, qg, kg):
    B, S, D = x.shape
    tm = TM_PROJ
    n_t = S // tm
    n_kc = S // TK
    sub = TK // tm
    in_cols = w_in.shape[1]
    tok = lambda width: pl.BlockSpec((None, tm, width), lambda b, i: (b, i, 0))
    out_shape = (
        jax.ShapeDtypeStruct((B, S, 2 * SG_WIDTH), BF16),
        jax.ShapeDtypeStruct((B, DA_HEADS, S // TQ, DA_V_DIM, TQ), BF16),
        jax.ShapeDtypeStruct((B, DA_HEADS, S, DA_V_DIM), BF16),
        jax.ShapeDtypeStruct((B, DA_HEADS, n_kc, DA_V_DIM, TK), BF16),
        jax.ShapeDtypeStruct((B, S, 2 * D_MODEL), BF16),
    )
    out_specs = (
        tok(2 * SG_WIDTH),
        pl.BlockSpec((None, DA_HEADS, None, DA_V_DIM, tm), lambda b, i: (b, 0, i // sub, 0, i % sub)),
        pl.BlockSpec((None, DA_HEADS, tm, DA_V_DIM), lambda b, i: (b, 0, i, 0)),
        pl.BlockSpec((None, DA_HEADS, None, DA_V_DIM, tm), lambda b, i: (b, 0, i // sub, 0, i % sub)),
        tok(2 * D_MODEL),
    )
    in_specs = [
        tok(D),
        _const_spec((1, D)),
        _const_spec((D, in_cols)),
        _const_spec((1, DA_WIDTH)),
        _const_spec((1, DA_WIDTH)),
    ]
    return pl.pallas_call(
        _inproj_kernel,
        grid=(B, n_t),
        in_specs=in_specs,
        out_specs=out_specs,
        out_shape=out_shape,
        scratch_shapes=[pltpu.VMEM((tm, D), BF16)],
        compiler_params=pltpu.CompilerParams(vmem_limit_bytes=VMEM_LIMIT_BYTES),
        name="in_projection",
    )(x, n1g, w_in, qg, kg)


def _split3(x):
    hi = x.astype(BF16).astype(F32)
    r = x - hi
    mid = r.astype(BF16).astype(F32)
    lo = (r - mid).astype(BF16).astype(F32)
    return hi, mid, lo


def _alibi_constants(slopes, seq, tq):
    parts = jnp.stack(_split3(slopes), axis=0)
    idx = jnp.arange(LANES, dtype=jnp.int32)
    part_of = parts[idx % 3].T
    kpos = jnp.arange(seq, dtype=jnp.int32)[None, :, None]
    a = idx[None, None, :]
    kcol = jnp.where(a < 3, -1.0, jnp.where(a < 9, -part_of[:, None, :],
           jnp.where(a < 12, ((kpos // POS_SPLIT) * POS_SPLIT).astype(F32),
           jnp.where(a < N_AUG, (kpos % POS_SPLIT).astype(F32), 0.0))))
    qloc = (jnp.arange(2 * tq, dtype=jnp.int32) % tq)[None, None, :]
    a = jnp.arange(BF16_SUBLANES, dtype=jnp.int32)[None, :, None]
    qrow = jnp.where((a >= 3) & (a < 6), ((qloc // POS_SPLIT) * POS_SPLIT).astype(F32),
           jnp.where((a >= 6) & (a < 9), (qloc % POS_SPLIT).astype(F32),
           jnp.where((a >= 9) & (a < N_AUG), part_of[:, :BF16_SUBLANES, None], 0.0)))
    return kcol.astype(BF16), qrow


def _attn_kernel(slopes_ref, qt_ref, k_ref, kcol_ref, qrow_ref, dist_ref, vt_ref,
                 lq1_ref, lk1_ref, lq2_ref, lk2_ref, sg_ref, o_ref,
                 qmain_ref, qaug_ref, acc_ref, kmax_ref, fast_ref):
    h = pl.program_id(1)
    n_q, dv, tq = qt_ref.shape
    n_kc, _, tk = vt_ref.shape
    slope = slopes_ref[h]

    def rows_of(index, size):
        start = index * size
        return pl.ds(start if isinstance(start, int) else pl.multiple_of(start, size), size)

    def k_chunk(j):
        return k_ref[rows_of(j, tk), :]

    qaug_ref[:, :, BF16_SUBLANES:, :] = jnp.zeros((2, 3, LANES - BF16_SUBLANES, 2 * tq), BF16)
    ones_bd = _group_ones(LANES)
    best = jnp.zeros((1, LANES), F32)
    for c in range(n_kc):
        kf = k_ref[c * tk:(c + 1) * tk, :].astype(F32)
        best = jnp.maximum(best, jnp.max(_dot((kf * kf).astype(BF16), ones_bd), axis=0, keepdims=True))
    kmax_ref[0:1, :] = jnp.sqrt(best) * KMAX_MARGIN
    lam = (jnp.exp(jnp.sum(lq1_ref[...] * lk1_ref[...], axis=-1, keepdims=True))
           - jnp.exp(jnp.sum(lq2_ref[...] * lk2_ref[...], axis=-1, keepdims=True)) + LAMBDA_INIT)

    def prepare(blk, slot):
        qt = qt_ref[blk]
        sub = lax.broadcasted_iota(jnp.int32, (dv, tq), 0)
        zero = jnp.zeros_like(qt)
        qmain_ref[slot, :, 0:tq] = jnp.where(sub < DA_HEAD_DIM, qt, zero)
        qmain_ref[slot, :, tq:2 * tq] = jnp.where(sub < DA_HEAD_DIM, zero, qt)
        qsq = qt.astype(F32) * qt.astype(F32)
        n1 = jnp.sqrt(jnp.sum(qsq[:DA_HEAD_DIM], axis=0, keepdims=True))
        n2 = jnp.sqrt(jnp.sum(qsq[DA_HEAD_DIM:], axis=0, keepdims=True))
        m = jnp.concatenate([n1 * kmax_ref[0:1, 0:1], n2 * kmax_ref[0:1, DA_HEAD_DIM:DA_HEAD_DIM + 1]], axis=1)
        a = lax.broadcasted_iota(jnp.int32, (BF16_SUBLANES, 2 * tq), 0)
        positional = qrow_ref[...] + jnp.where((a >= 3) & (a < 6), jnp.asarray(blk * tq, F32), 0.0)
        m_hi, m_mid, m_lo = _split3(m)
        m_rows = jnp.where(a == 0, m_hi, jnp.where(a == 1, m_mid, m_lo))
        for variant, tail in enumerate((positional, -positional, jnp.zeros_like(positional))):
            qaug_ref[slot, variant, 0:BF16_SUBLANES, :] = jnp.where(a < 3, m_rows, tail).astype(BF16)
        fast_ref[slot] = jnp.where(jnp.max(m) <= SAFE_EXP2_RANGE, 1, 0)

    def finish(blk, slot):
        o = acc_ref[slot, 0:dv, :] / acc_ref[slot, dv:dv + 1, :]
        o = o[:, :tq] - lam * o[:, tq:]
        ms = jnp.mean(o * o, axis=0, keepdims=True)
        o = (o * lax.rsqrt(ms + EPS)).T
        o_ref[rows_of(blk, tq), :] = (o * sg_ref[...] * (1.0 - LAMBDA_INIT)).astype(BF16)

    def scores_fast(blk, slot):
        ones_row = jnp.where(lax.broadcasted_iota(jnp.int32, (BF16_SUBLANES, tk), 0) == 0, 1.0, 0.0).astype(BF16)

        def chunk(j, variant, bias):
            k_aug = jnp.concatenate([k_chunk(j), kcol_ref[rows_of(j, tk), :]], axis=1)
            e = _dot(k_aug, jnp.concatenate([qmain_ref[slot], qaug_ref[slot, variant]], axis=0))
            if bias is not None:
                e = e - bias
            p = jnp.exp2(e).astype(BF16)
            return _dot(jnp.concatenate([vt_ref[j], ones_row], axis=0), p)

        diag_bias = dist_ref[...] * slope
        acc = chunk(blk, 2, jnp.concatenate([diag_bias, diag_bias], axis=1))
        for t in range(n_kc - 1):
            j = t + jnp.where(t >= blk, 1, 0)
            acc = acc + chunk(j, jnp.where(j > blk, 1, 0), None)
        acc_ref[slot] = acc

    def scores_online(blk, slot):
        signed = (lax.broadcasted_iota(jnp.int32, (tk, 2 * tq), 1) % tq
                  - lax.broadcasted_iota(jnp.int32, (tk, 2 * tq), 0)).astype(F32) * slope

        def body(j, carry):
            mx, l, acc = carry
            off = jnp.asarray(blk * tq - j * tk, F32) * slope
            s = _dot(k_chunk(j), qmain_ref[slot]) - jnp.abs(signed + off)
            m_new = jnp.maximum(mx, jnp.max(s, axis=0, keepdims=True))
            alpha = jnp.exp2(mx - m_new)
            p = jnp.exp2(s - m_new)
            l = alpha * l + jnp.sum(p, axis=0, keepdims=True)
            acc = alpha * acc + _dot(vt_ref[j], p.astype(BF16))
            return m_new, l, acc

        m0 = jnp.full((1, 2 * tq), -jnp.inf, F32)
        l0 = jnp.zeros((1, 2 * tq), F32)
        a0 = jnp.zeros((dv, 2 * tq), F32)
        _, l, acc = lax.fori_loop(0, n_kc, body, (m0, l0, a0))
        acc_ref[slot, 0:dv, :] = acc
        acc_ref[slot, dv:dv + BF16_SUBLANES, :] = jnp.broadcast_to(l, (BF16_SUBLANES, 2 * tq))

    def stage(blk, slot):
        other = 1 - slot
        prev = jnp.maximum(blk - 1, 0)
        nxt = jnp.minimum(blk + 1, n_q - 1)
        fast = fast_ref[slot] == 1

        @pl.when(fast)
        def _():
            finish(prev, other)
            scores_fast(blk, slot)
            prepare(nxt, other)

        @pl.when(jnp.logical_not(fast))
        def _():
            finish(prev, other)
            scores_online(blk, slot)
            prepare(nxt, other)

    acc_ref[1] = jnp.ones(acc_ref.shape[1:], F32)
    prepare(0, 0)

    def pair(t, carry):
        stage(2 * t, 0)
        stage(2 * t + 1, 1)
        return carry

    lax.fori_loop(0, n_q // 2, pair, 0)
    finish(n_q - 1, 1)


def _attention(slopes, qt, k, vt, lq1, lk1, lq2, lk2, subln_g):
    B, H, n_q, dv, tq = qt.shape
    n_kc = vt.shape[2]
    S = k.shape[2]
    assert n_q % 2 == 0
    kcol, qrow = _alibi_constants(slopes, S, tq)
    dist = jnp.abs(jnp.arange(TK, dtype=jnp.int32)[:, None]
                   - jnp.arange(tq, dtype=jnp.int32)[None, :]).astype(F32)
    small = lambda n: pl.BlockSpec((1, n), lambda b, h, *_: (0, 0))
    grid_spec = pltpu.PrefetchScalarGridSpec(
        num_scalar_prefetch=1,
        grid=(B, H),
        in_specs=[
            pl.BlockSpec((None, None, n_q, dv, tq), lambda b, h, *_: (b, h, 0, 0, 0)),
            pl.BlockSpec((None, None, S, dv), lambda b, h, *_: (b, h, 0, 0)),
            pl.BlockSpec((None, S, LANES), lambda b, h, *_: (h, 0, 0)),
            pl.BlockSpec((None, BF16_SUBLANES, 2 * tq), lambda b, h, *_: (h, 0, 0)),
            pl.BlockSpec((TK, tq), lambda b, h, *_: (0, 0)),
            pl.BlockSpec((None, None, n_kc, dv, TK), lambda b, h, *_: (b, h, 0, 0, 0)),
            small(DA_HEAD_DIM), small(DA_HEAD_DIM), small(DA_HEAD_DIM), small(DA_HEAD_DIM),
            small(DA_V_DIM),
        ],
        out_specs=pl.BlockSpec((None, S, dv), lambda b, h, *_: (b, 0, h)),
        scratch_shapes=[
            pltpu.VMEM((2, dv, 2 * tq), BF16),
            pltpu.VMEM((2, 3, LANES, 2 * tq), BF16),
            pltpu.VMEM((2, dv + BF16_SUBLANES, 2 * tq), F32),
            pltpu.VMEM((8, LANES), F32),
            pltpu.SMEM((2,), jnp.int32),
        ],
    )
    return pl.pallas_call(
        _attn_kernel,
        grid_spec=grid_spec,
        out_shape=jax.ShapeDtypeStruct((B, S, H * dv), BF16),
        compiler_params=pltpu.CompilerParams(vmem_limit_bytes=VMEM_LIMIT_BYTES),
        name="diff_attention",
    )(slopes, qt, k, kcol, qrow, dist, vt, lq1, lk1, lq2, lk2, subln_g)


def _merge_kernel(x_ref, uv_ref, oda_ref, gate_ref, bg_ref, lng_ref, lnb_ref, sw_ref, sb_ref,
                  psg_ref, pda_ref, wo_ref, h_ref, vn_ref):
    tm = x_ref.shape[0]
    n_chunks = tm // CHUNK

    y_da = _dot(oda_ref[...], pda_ref[...])

    gu = jax.nn.gelu(uv_ref[:, :SG_WIDTH].astype(F32))
    gv = jax.nn.gelu(uv_ref[:, SG_WIDTH:].astype(F32))
    mu = jnp.mean(gv, axis=-1, keepdims=True)
    cen = gv - mu
    var = jnp.mean(cen * cen, axis=-1, keepdims=True)
    vn_ref[...] = (cen * lax.rsqrt(var + EPS) * lng_ref[...] + lnb_ref[...]).astype(BF16)

    lane = lax.broadcasted_iota(jnp.int32, (CHUNK, tm), 1)
    first_group = (lane % LANES) < SG_GROUP_DIM
    mixed = []
    for pr in range(SG_GROUPS // 2):
        vp = jnp.concatenate([vn_ref[c * CHUNK:(c + 1) * CHUNK, pr * LANES:(pr + 1) * LANES]
                              for c in range(n_chunks)], axis=1)
        mixed.append(jnp.where(first_group, _dot(sw_ref[2 * pr], vp), _dot(sw_ref[2 * pr + 1], vp)))
    bias = sb_ref[...]
    sv = jnp.concatenate(
        [jnp.concatenate([m[:, c * LANES:(c + 1) * LANES] for m in mixed], axis=1) + bias
         for c in range(n_chunks)], axis=0)
    o_sg = (gu * sv).astype(BF16)

    y_sg = _dot(o_sg, psg_ref[...])
    g_sg =jax.nn.sigmoid(gate_ref[:, :D_MODEL].astype(F32) + bg_ref[:, :D_MODEL])
    g_da = jax.nn.sigmoid(gate_ref[:, D_MODEL:].astype(F32) + bg_ref[:, D_MODEL:])
    z = (g_sg * y_sg + g_da * y_da).astype(BF16)
    h_ref[...] = x_ref[...] + _dot(z, wo_ref[...])


def _merge(x, uv, o_da, gate_pre, b_gate, ln_g, ln_b, sg_w, sg_bias, p_sg, p_da, w_out):
    B, S, D = x.shape
    tm = TM_MERGE
    tok = lambda width: pl.BlockSpec((None, tm, width), lambda b, i: (b, i, 0))
    return pl.pallas_call(
        _merge_kernel,
        grid=(B, S // tm),
        in_specs=[
            tok(D), tok(2 * SG_WIDTH), tok(DA_WIDTH), tok(2 * D_MODEL),
            _const_spec(b_gate.shape), _const_spec(ln_g.shape), _const_spec(ln_b.shape),
            _const_spec(sg_w.shape), _const_spec(sg_bias.shape),
            _const_spec(p_sg.shape), _const_spec(p_da.shape), _const_spec(w_out.shape),
        ],
        out_specs=tok(D),
        out_shape=jax.ShapeDtypeStruct((B, S, D), F32),
        scratch_shapes=[pltpu.VMEM((tm, SG_WIDTH), BF16)],
        compiler_params=pltpu.CompilerParams(vmem_limit_bytes=VMEM_LIMIT_BYTES),
        name="merge",
    )(x, uv, o_da, gate_pre, b_gate, ln_g, ln_b, sg_w, sg_bias, p_sg, p_da, w_out)


def _ffn_kernel(h_ref, g_ref, wg_ref, wu_ref, wd_ref, o_ref):
    hres = h_ref[...]
    ms = jnp.mean(hres * hres, axis=-1, keepdims=True)
    hn = (hres * lax.rsqrt(ms + EPS) * g_ref[...]).astype(BF16)
    d_ff = wg_ref.shape[1]
    acc = hres
    for c in range(d_ff // FF_CHUNK):
        sl = slice(c * FF_CHUNK, (c + 1) * FF_CHUNK)
        a = jax.nn.silu(_dot(hn, wg_ref[:, sl])) * _dot(hn, wu_ref[:, sl])
        acc = acc + _dot(a.astype(BF16), wd_ref[sl, :])
    o_ref[...] = acc


def _ffn(h, g, w_gate, w_up, w_down):
    B, S, D = h.shape
    tm = TM_FFN
    tok = pl.BlockSpec((None, tm, D), lambda b, i: (b, i, 0))
    return pl.pallas_call(
        _ffn_kernel,
        grid=(B, S // tm),
        in_specs=[tok, _const_spec((1, D)), _const_spec(w_gate.shape), _const_spec(w_up.shape),
                  _const_spec(w_down.shape)],
        out_specs=tok,
        out_shape=jax.ShapeDtypeStruct((B, S, D), F32),
        compiler_params=pltpu.CompilerParams(vmem_limit_bytes=VMEM_LIMIT_BYTES),
        name="swiglu_ffn",
    )(h, g, w_gate, w_up, w_down)


def kernel(x, norm1_g, w_in, b_gate, sg_ln_g, sg_ln_b, sg_w, sg_b, q_norm_g, k_norm_g, lam_q1, lam_k1, lam_q2, lam_k2, subln_g, w_proj_sg, w_proj_da, w_out, norm2_g, w_ffn_gate, w_ffn_up, w_ffn_down):
    depth = w_in.shape[0]
    row = lambda a: a.reshape(1, -1).astype(F32)
    n_rep = DA_WIDTH // DA_HEAD_DIM
    slopes = jnp.asarray([2.0 ** (-8.0 * (i + 1) / DA_HEADS) * LOG2E for i in range(DA_HEADS)], F32)
    for l in range(depth):
        qg = row(jnp.tile(q_norm_g[l], n_rep)) * (LOG2E / math.sqrt(DA_HEAD_DIM))
        kg = row(jnp.tile(k_norm_g[l], n_rep))
        uv, qt, k, vt, gate_pre = _in_projection(x, row(norm1_g[l]), w_in[l].astype(BF16), qg, kg)
        o_da = _attention(slopes, qt, k, vt, row(lam_q1[l]), row(lam_k1[l]), row(lam_q2[l]),
                          row(lam_k2[l]), row(subln_g[l]))
        sg_bias = jnp.repeat(sg_b[l].T.astype(F32), SG_GROUP_DIM, axis=1)
        h = _merge(x, uv, o_da, gate_pre, row(b_gate[l]), row(sg_ln_g[l]), row(sg_ln_b[l]),
                   sg_w[l].astype(BF16), sg_bias,
                   w_proj_sg[l].astype(BF16), w_proj_da[l].astype(BF16), w_out[l].astype(BF16))
        x = _ffn(h, row(norm2_g[l]), w_ffn_gate[l].astype(BF16), w_ffn_up[l].astype(BF16),
                 w_ffn_down[l].astype(BF16))
    return x
```

```python
import math

import jax
import jax.numpy as jnp
from jax import lax
from jax.experimental import pallas as pl
from jax.experimental.pallas import tpu as pltpu

D_MODEL = 1024
SG_GROUPS = 8
SG_GROUP_DIM = 64
SG_WIDTH = SG_GROUPS * SG_GROUP_DIM
CHUNK = 128
DA_HEADS = 8
DA_HEAD_DIM = 64
DA_V_DIM = 2 * DA_HEAD_DIM
DA_WIDTH = DA_HEADS * DA_V_DIM
EPS = 1e-6
LAMBDA_INIT = 0.8 - 0.6 * math.exp(-0.3 * 0)
LOG2E = math.log2(math.e)

BF16 = jnp.bfloat16
F32 = jnp.float32

LANES = 128
F32_SUBLANES = 8
BF16_SUBLANES = 16
MXU_DIM = 256
VMEM_LIMIT_BYTES = 56 * 1024 * 1024

TM_PROJ = 512
TQ = 512
TK = 512
POS_SPLIT = 64
N_AUG = 15
SAFE_EXP2_RANGE = 60.0
KMAX_MARGIN = 1.01
BLOCKS_PER_REGION = 2
assert TQ == TK and N_AUG <= BF16_SUBLANES
TM_MERGE = 512
TM_FFN = 1024
FF_CHUNK = 256


def _dot(a, b):
    return jnp.dot(a, b, preferred_element_type=F32)


def _const_spec(shape):
    n = len(shape)
    return pl.BlockSpec(shape, lambda *_: (0,) * n, pipeline_mode=pl.Buffered(1))


def _group_ones(n, value=1.0):
    r = lax.broadcasted_iota(jnp.int32, (n, n), 0) // DA_HEAD_DIM
    c = lax.broadcasted_iota(jnp.int32, (n, n), 1) // DA_HEAD_DIM
    return jnp.where(r == c, value, 0.0).astype(BF16)


def _inproj_kernel(x_ref, n1g_ref, w_ref, qg_ref, kg_ref,
                   uv_ref, qt_ref, k_ref, vt_ref, gate_ref, xn_ref):
    x = x_ref[...]
    ms = jnp.mean(x * x, axis=-1, keepdims=True)
    xn_ref[...] = (x * lax.rsqrt(ms + EPS) * n1g_ref[...]).astype(BF16)
    averager = _group_ones(MXU_DIM, 1.0 / DA_HEAD_DIM)

    def unit_rms(p, gain_ref):
        sq = (p * p).astype(BF16)
        mean_sq = jnp.concatenate([_dot(sq[:, i * MXU_DIM:(i + 1) * MXU_DIM], averager)
                                   for i in range(p.shape[1] // MXU_DIM)], axis=1)
        return p * lax.rsqrt(mean_sq + EPS) * gain_ref[...]

    def spatial(p):
        uv_ref[...] = p.astype(BF16)

    def query(p):
        qn = unit_rms(p, qg_ref)
        for h in range(DA_HEADS):
            qt_ref[h] = qn[:, h * DA_V_DIM:(h + 1) * DA_V_DIM].T.astype(BF16)

    def key(p):
        kn = unit_rms(p, kg_ref).astype(BF16)
        for h in range(DA_HEADS):
            k_ref[h] = kn[:, h * DA_V_DIM:(h + 1) * DA_V_DIM]

    def value(p):
        for h in range(DA_HEADS):
            vt_ref[h] = p[:, h * DA_V_DIM:(h + 1) * DA_V_DIM].T.astype(BF16)

    def gate_sg(p):
        gate_ref[:, :D_MODEL] = p.astype(BF16)

    def gate_da(p):
        gate_ref[:, D_MODEL:] = p.astype(BF16)

    col = 0
    for width, epilogue in ((2 * SG_WIDTH, spatial), (DA_WIDTH, query), (DA_WIDTH, key), (DA_WIDTH, value),
                            (D_MODEL, gate_sg), (D_MODEL, gate_da)):
        epilogue(_dot(xn_ref[...], w_ref[:, col:col + width]))
        col += width


def _in_projection(x, n1g, w_in, qg, kg):
    B, S, D = x.shape
    tm = TM_PROJ
    n_t = S // tm
    n_kc = S // TK
    sub = TK // tm
    in_cols = w_in.shape[1]
    tok = lambda width: pl.BlockSpec((None, tm, width), lambda b, i: (b, i, 0))
    out_shape = (
        jax.ShapeDtypeStruct((B, S, 2 * SG_WIDTH), BF16),
        jax.ShapeDtypeStruct((B, DA_HEADS, S // TQ, DA_V_DIM, TQ), BF16),
        jax.ShapeDtypeStruct((B, DA_HEADS, S, DA_V_DIM), BF16),
        jax.ShapeDtypeStruct((B, DA_HEADS, n_kc, DA_V_DIM, TK), BF16),
        jax.ShapeDtypeStruct((B, S, 2 * D_MODEL), BF16),
    )
    out_specs = (
        tok(2 * SG_WIDTH),
        pl.BlockSpec((None, DA_HEADS, None, DA_V_DIM, tm), lambda b, i: (b, 0, i // sub, 0, i % sub)),
        pl.BlockSpec((None, DA_HEADS, tm, DA_V_DIM), lambda b, i: (b, 0, i, 0)),
        pl.BlockSpec((None, DA_HEADS, None, DA_V_DIM, tm), lambda b, i: (b, 0, i // sub, 0, i % sub)),
        tok(2 * D_MODEL),
    )
    in_specs = [
        tok(D),
        _const_spec((1, D)),
        _const_spec((D, in_cols)),
        _const_spec((1, DA_WIDTH)),
        _const_spec((1, DA_WIDTH)),
    ]
    return pl.pallas_call(
        _inproj_kernel,
        grid=(B, n_t),
        in_specs=in_specs,
        out_specs=out_specs,
        out_shape=out_shape,
        scratch_shapes=[pltpu.VMEM((tm, D), BF16)],
        compiler_params=pltpu.CompilerParams(vmem_limit_bytes=VMEM_LIMIT_BYTES),
        name="in_projection",
    )(x, n1g, w_in, qg, kg)


def _split3(x):
    hi = x.astype(BF16).astype(F32)
    r = x - hi
    mid = r.astype(BF16).astype(F32)
    lo = (r - mid).astype(BF16).astype(F32)
    return hi, mid, lo


def _alibi_constants(slopes, seq, tq):
    parts = jnp.stack(_split3(slopes), axis=0)
    idx = jnp.arange(LANES, dtype=jnp.int32)
    part_of = parts[idx % 3].T
    kpos = jnp.arange(seq, dtype=jnp.int32)[None, :, None]
    a = idx[None, None, :]
    kcol = jnp.where(a < 3, -1.0, jnp.where(a < 9, -part_of[:, None, :],
           jnp.where(a < 12, ((kpos // POS_SPLIT) * POS_SPLIT).astype(F32),
           jnp.where(a < N_AUG, (kpos % POS_SPLIT).astype(F32), 0.0))))
    qloc = (jnp.arange(2 * tq, dtype=jnp.int32) % tq)[None, None, :]
    a = jnp.arange(BF16_SUBLANES, dtype=jnp.int32)[None, :, None]
    qrow = jnp.where((a >= 3) & (a < 6), ((qloc // POS_SPLIT) * POS_SPLIT).astype(F32),
           jnp.where((a >= 6) & (a < 9), (qloc % POS_SPLIT).astype(F32),
           jnp.where((a >= 9) & (a < N_AUG), part_of[:, :BF16_SUBLANES, None], 0.0)))
    return kcol.astype(BF16), qrow


def _attn_kernel(slopes_ref, qt_ref, k_ref, kcol_ref, qrow_ref, dist_ref, vt_ref,
                 lq1_ref, lk1_ref, lq2_ref, lk2_ref, sg_ref, o_ref,
                 qmain_ref, qaug_ref, acc_ref, kmax_ref, fast_ref):
    h = pl.program_id(1)
    n_q, dv, tq = qt_ref.shape
    n_kc, _, tk = vt_ref.shape
    slope = slopes_ref[h]

    def rows_of(index, size):
        start = index * size
        return pl.ds(start if isinstance(start, int) else pl.multiple_of(start, size), size)

    def k_chunk(j):
        return k_ref[rows_of(j, tk), :]

    @pl.when((pl.program_id(0) == 0) & (h == 0))
    def _():
        qaug_ref[:, :, BF16_SUBLANES:, :] = jnp.zeros(
            (qaug_ref.shape[0], 3, LANES - BF16_SUBLANES, 2 * tq), BF16)

    ones_bd = _group_ones(LANES)
    best = jnp.zeros((1, LANES), F32)
    for c in range(n_kc):
        kf = k_ref[c * tk:(c + 1) * tk, :].astype(F32)
        best = jnp.maximum(best, jnp.max(_dot((kf * kf).astype(BF16), ones_bd), axis=0, keepdims=True))
    kmax_ref[0:1, :] = jnp.sqrt(best) * KMAX_MARGIN
    lam = (jnp.exp(jnp.sum(lq1_ref[...] * lk1_ref[...], axis=-1, keepdims=True))
           - jnp.exp(jnp.sum(lq2_ref[...] * lk2_ref[...], axis=-1, keepdims=True)) + LAMBDA_INIT)

    def prepare(blk, slot):
        qt = qt_ref[blk]
        sub = lax.broadcasted_iota(jnp.int32, (dv, tq), 0)
        zero = jnp.zeros_like(qt)
        qmain_ref[slot, :, 0:tq] = jnp.where(sub < DA_HEAD_DIM, qt, zero)
        qmain_ref[slot, :, tq:2 * tq] = jnp.where(sub < DA_HEAD_DIM, zero, qt)
        qsq = qt.astype(F32) * qt.astype(F32)
        n1 = jnp.sqrt(jnp.sum(qsq[:DA_HEAD_DIM], axis=0, keepdims=True))
        n2 = jnp.sqrt(jnp.sum(qsq[DA_HEAD_DIM:], axis=0, keepdims=True))
        m = jnp.concatenate([n1 * kmax_ref[0:1, 0:1], n2 * kmax_ref[0:1, DA_HEAD_DIM:DA_HEAD_DIM + 1]], axis=1)
        a = lax.broadcasted_iota(jnp.int32, (BF16_SUBLANES, 2 * tq), 0)
        positional = qrow_ref[...] + jnp.where((a >= 3) & (a < 6), jnp.asarray(blk * tq, F32), 0.0)
        m_hi, m_mid, m_lo = _split3(m)
        m_rows = jnp.where(a == 0, m_hi, jnp.where(a == 1, m_mid, m_lo))
        for variant, tail in enumerate((positional, -positional, jnp.zeros_like(positional))):
            qaug_ref[slot, variant, 0:BF16_SUBLANES, :] = jnp.where(a < 3, m_rows, tail).astype(BF16)
        fast_ref[slot] = jnp.where(jnp.max(m) <= SAFE_EXP2_RANGE, 1, 0)

    def finish(blk, slot):
        o = acc_ref[slot, 0:dv, :] / jnp.sum(acc_ref[slot, dv:dv + F32_SUBLANES, :], axis=0, keepdims=True)
        o = o[:, :tq] - lam * o[:, tq:]
        ms = jnp.mean(o * o, axis=0, keepdims=True)
        o = (o * lax.rsqrt(ms + EPS)).T
        o_ref[rows_of(blk, tq), :] = (o * sg_ref[...] * (1.0 - LAMBDA_INIT)).astype(BF16)

    def scores_fast(blk, slot):
        def chunk(j, variant, bias):
            k_aug = jnp.concatenate([k_chunk(j), kcol_ref[rows_of(j, tk), :]], axis=1)
            e = _dot(k_aug, jnp.concatenate([qmain_ref[slot], qaug_ref[slot, variant]], axis=0))
            if bias is not None:
                e = e - bias
            p = jnp.exp2(e)
            denom = jnp.sum(p.reshape(tk // F32_SUBLANES, F32_SUBLANES, 2 * tq), axis=0)
            return _dot(vt_ref[j], p.astype(BF16)), denom

        diag_bias = dist_ref[...] * slope
        acc, denom = chunk(blk, 2, jnp.concatenate([diag_bias, diag_bias], axis=1))
        for t in range(n_kc - 1):
            j = t + jnp.where(t >= blk, 1, 0)
            pv, d = chunk(j, jnp.where(j > blk, 1, 0), None)
            acc, denom = acc + pv, denom + d
        acc_ref[slot, 0:dv, :] = acc
        acc_ref[slot, dv:dv + F32_SUBLANES, :] = denom

    def scores_online(blk, slot):
        signed =(lax.broadcasted_iota(jnp.int32, (tk, 2 * tq), 1) % tq
                  - lax.broadcasted_iota(jnp.int32, (tk, 2 * tq), 0)).astype(F32) * slope

        def body(j, carry):
            mx, l, acc = carry
            off = jnp.asarray(blk * tq - j * tk, F32) * slope
            s = _dot(k_chunk(j), qmain_ref[slot]) - jnp.abs(signed + off)
            m_new = jnp.maximum(mx, jnp.max(s, axis=0, keepdims=True))
            alpha = jnp.exp2(mx - m_new)
            p = jnp.exp2(s - m_new)
            l = alpha * l + jnp.sum(p, axis=0, keepdims=True)
            acc = alpha * acc + _dot(vt_ref[j], p.astype(BF16))
            return m_new, l, acc

        m0 = jnp.full((1, 2 * tq), -jnp.inf, F32)
        l0 = jnp.zeros((1, 2 * tq), F32)
        a0 = jnp.zeros((dv, 2 * tq), F32)
        _, l, acc = lax.fori_loop(0, n_kc, body, (m0, l0, a0))
        acc_ref[slot, 0:dv, :] = acc
        acc_ref[slot, dv:dv + F32_SUBLANES, :] = jnp.broadcast_to(l * (1.0 / F32_SUBLANES), (F32_SUBLANES, 2 * tq))

    group = BLOCKS_PER_REGION

    def region(first_blk, cur, oth):
        blks = [first_blk + g for g in range(group)]
        prev = [jnp.maximum(b - group, g) for g, b in enumerate(blks)]
        nxt = [jnp.minimum(b + group, n_q - group + g) for g, b in enumerate(blks)]
        fast = fast_ref[cur[0]] == 1
        for s in cur[1:]:
            fast = fast & (fast_ref[s] == 1)

        def body(scores):
            for g in range(group):
                finish(prev[g], oth[g])
            for g in range(group):
                scores(blks[g], cur[g])
            for g in range(group):
                prepare(nxt[g], oth[g])

        pl.when(fast)(lambda: body(scores_fast))
        pl.when(jnp.logical_not(fast))(lambda: body(scores_online))

    slots_a = tuple(range(group))
    slots_b = tuple(range(group, 2 * group))
    for g in range(group):
        acc_ref[slots_b[g]] = jnp.ones(acc_ref.shape[1:], F32)
        prepare(g, slots_a[g])

    def two_regions(u, carry):
        region(2 * group * u, slots_a, slots_b)
        region(2 * group * u + group, slots_b, slots_a)
        return carry

    lax.fori_loop(0, n_q // (2 * group), two_regions, 0)
    for g in range(group):
        finish(n_q - group + g, slots_b[g])


def _attention(slopes, qt, k, vt, lq1, lk1, lq2, lk2, subln_g):
    B, H, n_q, dv, tq = qt.shape
    n_kc = vt.shape[2]
    S = k.shape[2]
    n_slots = 2 * BLOCKS_PER_REGION
    assert n_q % n_slots == 0
    kcol, qrow = _alibi_constants(slopes, S, tq)
    dist = jnp.abs(jnp.arange(TK, dtype=jnp.int32)[:, None]
                   - jnp.arange(tq, dtype=jnp.int32)[None, :]).astype(F32)
    small = lambda n: pl.BlockSpec((1, n), lambda b, h, *_: (0, 0))
    grid_spec = pltpu.PrefetchScalarGridSpec(
        num_scalar_prefetch=1,
        grid=(B, H),
        in_specs=[
            pl.BlockSpec((None, None, n_q, dv, tq), lambda b, h, *_: (b, h, 0, 0, 0)),
            pl.BlockSpec((None, None, S, dv), lambda b, h, *_: (b, h, 0, 0)),
            pl.BlockSpec((None, S, LANES), lambda b, h, *_: (h, 0, 0)),
            pl.BlockSpec((None, BF16_SUBLANES, 2 * tq), lambda b, h, *_: (h, 0, 0)),
            pl.BlockSpec((TK, tq), lambda b, h, *_: (0, 0)),
            pl.BlockSpec((None, None, n_kc, dv, TK), lambda b, h, *_: (b, h, 0, 0, 0)),
            small(DA_HEAD_DIM), small(DA_HEAD_DIM), small(DA_HEAD_DIM), small(DA_HEAD_DIM),
            small(DA_V_DIM),
        ],
        out_specs=pl.BlockSpec((None, S, dv), lambda b, h, *_: (b, 0, h)),
        scratch_shapes=[
            pltpu.VMEM((n_slots, dv, 2 * tq), BF16),
            pltpu.VMEM((n_slots, 3, LANES, 2 * tq), BF16),
            pltpu.VMEM((n_slots, dv + F32_SUBLANES, 2 * tq), F32),
            pltpu.VMEM((8, LANES), F32),
            pltpu.SMEM((n_slots,), jnp.int32),
        ],
    )
    return pl.pallas_call(
        _attn_kernel,
        grid_spec=grid_spec,
        out_shape=jax.ShapeDtypeStruct((B, S, H * dv), BF16),
        compiler_params=pltpu.CompilerParams(vmem_limit_bytes=VMEM_LIMIT_BYTES),
        name="diff_attention",
    )(slopes, qt, k, kcol, qrow, dist, vt, lq1, lk1, lq2, lk2, subln_g)


def _merge_kernel(x_ref, uv_ref, oda_ref, gate_ref, bg_ref, lng_ref, lnb_ref, sw_ref, sb_ref,
                  psg_ref, pda_ref, wo_ref, h_ref, vn_ref):
    tm = x_ref.shape[0]
    n_chunks = tm // CHUNK

    y_da = _dot(oda_ref[...], pda_ref[...])

    gu = jax.nn.gelu(uv_ref[:, :SG_WIDTH].astype(F32))
    gv = jax.nn.gelu(uv_ref[:, SG_WIDTH:].astype(F32))
    mu = jnp.mean(gv, axis=-1, keepdims=True)
    cen = gv - mu
    var = jnp.mean(cen * cen, axis=-1, keepdims=True)
    vn_ref[...] = (cen * lax.rsqrt(var + EPS) * lng_ref[...] + lnb_ref[...]).astype(BF16)

    lane = lax.broadcasted_iota(jnp.int32, (CHUNK, tm), 1)
    first_group = (lane % LANES) < SG_GROUP_DIM
    mixed = []
    for pr in range(SG_GROUPS // 2):
        vp = jnp.concatenate([vn_ref[c * CHUNK:(c + 1) * CHUNK, pr * LANES:(pr + 1) * LANES]
                              for c in range(n_chunks)], axis=1)
        mixed.append(jnp.where(first_group, _dot(sw_ref[2 * pr], vp), _dot(sw_ref[2 * pr + 1], vp)))
    bias = sb_ref[...]
    sv = jnp.concatenate(
        [jnp.concatenate([m[:, c * LANES:(c + 1) * LANES] for m in mixed], axis=1) + bias
         for c in range(n_chunks)], axis=0)
    o_sg = (gu * sv).astype(BF16)

    y_sg = _dot(o_sg, psg_ref[...])
    g_sg =jax.nn.sigmoid(gate_ref[:, :D_MODEL].astype(F32) + bg_ref[:, :D_MODEL])
    g_da = jax.nn.sigmoid(gate_ref[:, D_MODEL:].astype(F32) + bg_ref[:, D_MODEL:])
    z = (g_sg * y_sg + g_da * y_da).astype(BF16)
    h_ref[...] = x_ref[...] + _dot(z, wo_ref[...])


def _merge(x, uv, o_da, gate_pre, b_gate, ln_g, ln_b, sg_w, sg_bias, p_sg, p_da, w_out):
    B, S, D = x.shape
    tm = TM_MERGE
    tok = lambda width: pl.BlockSpec((None, tm, width), lambda b, i: (b, i, 0))
    return pl.pallas_call(
        _merge_kernel,
        grid=(B, S // tm),
        in_specs=[
            tok(D), tok(2 * SG_WIDTH), tok(DA_WIDTH), tok(2 * D_MODEL),
            _const_spec(b_gate.shape), _const_spec(ln_g.shape), _const_spec(ln_b.shape),
            _const_spec(sg_w.shape), _const_spec(sg_bias.shape),
            _const_spec(p_sg.shape), _const_spec(p_da.shape), _const_spec(w_out.shape),
        ],
        out_specs=tok(D),
        out_shape=jax.ShapeDtypeStruct((B, S, D), F32),
        scratch_shapes=[pltpu.VMEM((tm, SG_WIDTH), BF16)],
        compiler_params=pltpu.CompilerParams(vmem_limit_bytes=VMEM_LIMIT_BYTES),
        name="merge",
    )(x, uv, o_da, gate_pre, b_gate, ln_g, ln_b, sg_w, sg_bias, p_sg, p_da, w_out)


def _ffn_kernel(h_ref, g_ref, wg_ref, wu_ref, wd_ref, o_ref):
    hres = h_ref[...]
    ms = jnp.mean(hres * hres, axis=-1, keepdims=True)
    hn = (hres * lax.rsqrt(ms + EPS) * g_ref[...]).astype(BF16)
    d_ff = wg_ref.shape[1]
    acc = hres
    for c in range(d_ff // FF_CHUNK):
        sl = slice(c * FF_CHUNK, (c + 1) * FF_CHUNK)
        a = jax.nn.silu(_dot(hn, wg_ref[:, sl])) * _dot(hn, wu_ref[:, sl])
        acc = acc + _dot(a.astype(BF16), wd_ref[sl, :])
    o_ref[...] = acc


def _ffn(h, g, w_gate, w_up, w_down):
    B, S, D = h.shape
    tm = TM_FFN
    tok = pl.BlockSpec((None, tm, D), lambda b, i: (b, i, 0))
    return pl.pallas_call(
        _ffn_kernel,
        grid=(B, S // tm),
        in_specs=[tok, _const_spec((1, D)), _const_spec(w_gate.shape), _const_spec(w_up.shape),
                  _const_spec(w_down.shape)],
        out_specs=tok,
        out_shape=jax.ShapeDtypeStruct((B, S, D), F32),
        compiler_params=pltpu.CompilerParams(vmem_limit_bytes=VMEM_LIMIT_BYTES),
        name="swiglu_ffn",
    )(h, g, w_gate, w_up, w_down)


def kernel(x, norm1_g, w_in, b_gate, sg_ln_g, sg_ln_b, sg_w, sg_b, q_norm_g, k_norm_g, lam_q1, lam_k1, lam_q2, lam_k2, subln_g, w_proj_sg, w_proj_da, w_out, norm2_g, w_ffn_gate, w_ffn_up, w_ffn_down):
    depth = w_in.shape[0]
    row = lambda a: a.reshape(1, -1).astype(F32)
    n_rep = DA_WIDTH // DA_HEAD_DIM
    slopes = jnp.asarray([2.0 ** (-8.0 * (i + 1) / DA_HEADS) * LOG2E for i in range(DA_HEADS)], F32)
    for l in range(depth):
        qg = row(jnp.tile(q_norm_g[l], n_rep)) * (LOG2E / math.sqrt(DA_HEAD_DIM))
        kg = row(jnp.tile(k_norm_g[l], n_rep))
        uv, qt, k, vt, gate_pre = _in_projection(x, row(norm1_g[l]), w_in[l].astype(BF16), qg, kg)
        o_da = _attention(slopes, qt, k, vt, row(lam_q1[l]), row(lam_k1[l]), row(lam_q2[l]),
                          row(lam_k2[l]), row(subln_g[l]))
        sg_bias = jnp.repeat(sg_b[l].T.astype(F32), SG_GROUP_DIM, axis=1)
        h = _merge(x, uv, o_da, gate_pre, row(b_gate[l]), row(sg_ln_g[l]), row(sg_ln_b[l]),
                   sg_w[l].astype(BF16), sg_bias,
                   w_proj_sg[l].astype(BF16), w_proj_da[l].astype(BF16), w_out[l].astype(BF16))
        x = _ffn(h, row(norm2_g[l]), w_ffn_gate[l].astype(BF16), w_ffn_up[l].astype(BF16),
                 w_ffn_down[l].astype(BF16))
    return x
```

```python
import math

import jax
import jax.numpy as jnp
from jax import lax
from jax.experimental import pallas as pl
from jax.experimental.pallas import tpu as pltpu

D_MODEL = 1024
SG_GROUPS = 8
SG_GROUP_DIM = 64
SG_WIDTH = SG_GROUPS * SG_GROUP_DIM
CHUNK = 128
DA_HEADS = 8
DA_HEAD_DIM = 64
DA_V_DIM = 2 * DA_HEAD_DIM
DA_WIDTH = DA_HEADS * DA_V_DIM
EPS = 1e-6
LAMBDA_INIT = 0.8 - 0.6 * math.exp(-0.3 * 0)
LOG2E = math.log2(math.e)

BF16 = jnp.bfloat16
F32 = jnp.float32

LANES = 128
BF16_SUBLANES = 16
MXU_DIM = 256
VMEM_LIMIT_BYTES = 56 * 1024 * 1024

TM_PROJ = 512
TQ = 512
TK = 512
POS_SPLIT = 64
N_AUG = 15
SAFE_EXP2_RANGE = 60.0
KMAX_MARGIN = 1.01
BLOCKS_PER_REGION = 2
assert TQ == TK and N_AUG <= BF16_SUBLANES
TM_MERGE = 512
TM_FFN = 1024
FF_CHUNK = 256


def _dot(a, b):
    return jnp.dot(a, b, preferred_element_type=F32)


def _const_spec(shape):
    n = len(shape)
    return pl.BlockSpec(shape, lambda *_: (0,) * n, pipeline_mode=pl.Buffered(1))


def _group_ones(n, value=1.0):
    r = lax.broadcasted_iota(jnp.int32, (n, n), 0) // DA_HEAD_DIM
    c = lax.broadcasted_iota(jnp.int32, (n, n), 1) // DA_HEAD_DIM
    return jnp.where(r == c, value, 0.0).astype(BF16)


def _inproj_kernel(x_ref, n1g_ref, w_ref, qg_ref, kg_ref,
                   uv_ref, qt_ref, k_ref, vt_ref, gate_ref, xn_ref):
    x = x_ref[...]
    ms = jnp.mean(x * x, axis=-1, keepdims=True)
    xn_ref[...] = (x * lax.rsqrt(ms + EPS) * n1g_ref[...]).astype(BF16)
    averager = _group_ones(MXU_DIM, 1.0 / DA_HEAD_DIM)

    def unit_rms(p, gain_ref):
        sq = (p * p).astype(BF16)
        mean_sq = jnp.concatenate([_dot(sq[:, i * MXU_DIM:(i + 1) * MXU_DIM], averager)
                                   for i in range(p.shape[1] // MXU_DIM)], axis=1)
        return p * lax.rsqrt(mean_sq + EPS) * gain_ref[...]

    def spatial(p):
        uv_ref[...] = p.astype(BF16)

    def query(p):
        qn = unit_rms(p, qg_ref)
        for h in range(DA_HEADS):
            qt_ref[h] = qn[:, h * DA_V_DIM:(h + 1) * DA_V_DIM].T.astype(BF16)

    def key(p):
        kn = unit_rms(p, kg_ref).astype(BF16)
        for h in range(DA_HEADS):
            k_ref[h] = kn[:, h * DA_V_DIM:(h + 1) * DA_V_DIM]

    def value(p):
        for h in range(DA_HEADS):
            vt_ref[h] = p[:, h * DA_V_DIM:(h + 1) * DA_V_DIM].T.astype(BF16)

    def gate_sg(p):
        gate_ref[:, :D_MODEL] = p.astype(BF16)

    def gate_da(p):
        gate_ref[:, D_MODEL:] = p.astype(BF16)

    col = 0
    for width, epilogue in ((2 * SG_WIDTH, spatial), (DA_WIDTH, query), (DA_WIDTH, key), (DA_WIDTH, value),
                            (D_MODEL, gate_sg), (D_MODEL, gate_da)):
        epilogue(_dot(xn_ref[...], w_ref[:, col:col + width]))
        col += width


def _in_projection(x, n1g, w_in, qg, kg):
    B, S, D = x.shape
    tm = TM_PROJ
    n_t = S // tm
    n_kc = S // TK
    sub = TK // tm
    in_cols = w_in.shape[1]
    tok = lambda width: pl.BlockSpec((None, tm, width), lambda b, i: (b, i, 0))
    out_shape = (
        jax.ShapeDtypeStruct((B, S, 2 * SG_WIDTH), BF16),
        jax.ShapeDtypeStruct((B, DA_HEADS, S // TQ, DA_V_DIM, TQ), BF16),
        jax.ShapeDtypeStruct((B, DA_HEADS, S, DA_V_DIM), BF16),
        jax.ShapeDtypeStruct((B, DA_HEADS, n_kc, DA_V_DIM, TK), BF16),
        jax.ShapeDtypeStruct((B, S, 2 * D_MODEL), BF16),
    )
    out_specs = (
        tok(2 * SG_WIDTH),
        pl.BlockSpec((None, DA_HEADS, None, DA_V_DIM, tm), lambda b, i: (b, 0, i // sub, 0, i % sub)),
        pl.BlockSpec((None, DA_HEADS, tm, DA_V_DIM), lambda b, i: (b, 0, i, 0)),
        pl.BlockSpec((None, DA_HEADS, None, DA_V_DIM, tm), lambda b, i: (b, 0, i // sub, 0, i % sub)),
        tok(2 * D_MODEL),
    )
    in_specs = [
        tok(D),
        _const_spec((1, D)),
        _const_spec((D, in_cols)),
        _const_spec((1, DA_WIDTH)),
        _const_spec((1, DA_WIDTH)),
    ]
    return pl.pallas_call(
        _inproj_kernel,
        grid=(B, n_t),
        in_specs=in_specs,
        out_specs=out_specs,
        out_shape=out_shape,
        scratch_shapes=[pltpu.VMEM((tm, D), BF16)],
        compiler_params=pltpu.CompilerParams(vmem_limit_bytes=VMEM_LIMIT_BYTES),
        name="in_projection",
    )(x, n1g, w_in, qg, kg)


def _split3(x):
    hi = x.astype(BF16).astype(F32)
    r = x - hi
    mid = r.astype(BF16).astype(F32)
    lo = (r - mid).astype(BF16).astype(F32)
    return hi, mid, lo


def _alibi_constants(slopes, seq, tq):
    parts = jnp.stack(_split3(slopes), axis=0)
    idx = jnp.arange(LANES, dtype=jnp.int32)
    part_of = parts[idx % 3].T
    kpos = jnp.arange(seq, dtype=jnp.int32)[None, :, None]
    a = idx[None, None, :]
    kcol = jnp.where(a < 3, -1.0, jnp.where(a < 9, -part_of[:, None, :],
           jnp.where(a < 12, ((kpos // POS_SPLIT) * POS_SPLIT).astype(F32),
           jnp.where(a < N_AUG, (kpos % POS_SPLIT).astype(F32), 0.0))))
    qloc = (jnp.arange(2 * tq, dtype=jnp.int32) % tq)[None, None, :]
    a = jnp.arange(BF16_SUBLANES, dtype=jnp.int32)[None, :, None]
    qrow = jnp.where((a >= 3) & (a < 6), ((qloc // POS_SPLIT) * POS_SPLIT).astype(F32),
           jnp.where((a >= 6) & (a < 9), (qloc % POS_SPLIT).astype(F32),
           jnp.where((a >= 9) & (a < N_AUG), part_of[:, :BF16_SUBLANES, None], 0.0)))
    return kcol.astype(BF16), qrow


def _attn_kernel(slopes_ref, qt_ref, k_ref, kcol_ref, qrow_ref, dist_ref, vt_ref,
                 lq1_ref, lk1_ref, lq2_ref, lk2_ref, sg_ref, o_ref,
                 qmain_ref, qaug_ref, acc_ref, kmax_ref, fast_ref):
    h = pl.program_id(1)
    n_q, dv, tq = qt_ref.shape
    n_kc, _, tk = vt_ref.shape
    slope = slopes_ref[h]

    def rows_of(index, size):
        start = index * size
        return pl.ds(start if isinstance(start, int) else pl.multiple_of(start, size), size)

    def k_chunk(j):
        return k_ref[rows_of(j, tk), :]

    @pl.when((pl.program_id(0) == 0) & (h == 0))
    def _():
        qaug_ref[:, :, BF16_SUBLANES:, :] = jnp.zeros(
            (qaug_ref.shape[0], 3, LANES - BF16_SUBLANES, 2 * tq), BF16)

    ones_bd = _group_ones(LANES)
    best = jnp.zeros((1, LANES), F32)
    for c in range(n_kc):
        kf = k_ref[c * tk:(c + 1) * tk, :].astype(F32)
        best = jnp.maximum(best, jnp.max(_dot((kf * kf).astype(BF16), ones_bd), axis=0, keepdims=True))
    kmax_ref[0:1, :] = jnp.sqrt(best) * KMAX_MARGIN
    lam = (jnp.exp(jnp.sum(lq1_ref[...] * lk1_ref[...], axis=-1, keepdims=True))
           - jnp.exp(jnp.sum(lq2_ref[...] * lk2_ref[...], axis=-1, keepdims=True)) + LAMBDA_INIT)

    def prepare(blk, slot):
        qt = qt_ref[blk]
        sub = lax.broadcasted_iota(jnp.int32, (dv, tq), 0)
        zero = jnp.zeros_like(qt)
        qmain_ref[slot, :, 0:tq] = jnp.where(sub < DA_HEAD_DIM, qt, zero)
        qmain_ref[slot, :, tq:2 * tq] = jnp.where(sub < DA_HEAD_DIM, zero, qt)
        qsq = qt.astype(F32) * qt.astype(F32)
        n1 = jnp.sqrt(jnp.sum(qsq[:DA_HEAD_DIM], axis=0, keepdims=True))
        n2 = jnp.sqrt(jnp.sum(qsq[DA_HEAD_DIM:], axis=0, keepdims=True))
        m = jnp.concatenate([n1 * kmax_ref[0:1, 0:1], n2 * kmax_ref[0:1, DA_HEAD_DIM:DA_HEAD_DIM + 1]], axis=1)
        a = lax.broadcasted_iota(jnp.int32, (BF16_SUBLANES, 2 * tq), 0)
        positional = qrow_ref[...] + jnp.where((a >= 3) & (a < 6), jnp.asarray(blk * tq, F32), 0.0)
        m_hi, m_mid, m_lo = _split3(m)
        m_rows = jnp.where(a == 0, m_hi, jnp.where(a == 1, m_mid, m_lo))
        for variant, tail in enumerate((positional, -positional, jnp.zeros_like(positional))):
            qaug_ref[slot, variant, 0:BF16_SUBLANES, :] = jnp.where(a < 3, m_rows, tail).astype(BF16)
        fast_ref[slot] = jnp.where(jnp.max(m) <= SAFE_EXP2_RANGE, 1, 0)

    def finish(blk, slot):
        o = acc_ref[slot, 0:dv, :] / acc_ref[slot, dv:dv + 1, :]
        o = o[:, :tq] - lam * o[:, tq:]
        ms = jnp.mean(o * o, axis=0, keepdims=True)
        o = (o * lax.rsqrt(ms + EPS)).T
        o_ref[rows_of(blk, tq), :] = (o * sg_ref[...] * (1.0 - LAMBDA_INIT)).astype(BF16)

    def scores_fast(blk, slot):
        ones_row = jnp.where(lax.broadcasted_iota(jnp.int32, (BF16_SUBLANES, tk), 0) == 0, 1.0, 0.0).astype(BF16)

        def chunk(j, variant, bias):
            k_aug = jnp.concatenate([k_chunk(j), kcol_ref[rows_of(j, tk), :]], axis=1)
            e = _dot(k_aug, jnp.concatenate([qmain_ref[slot], qaug_ref[slot, variant]], axis=0))
            if bias is not None:
                e = e - bias
            p = jnp.exp2(e).astype(BF16)
            return _dot(jnp.concatenate([vt_ref[j], ones_row], axis=0), p)

        diag_bias = dist_ref[...] * slope
        acc = chunk(blk, 2, jnp.concatenate([diag_bias, diag_bias], axis=1))
        for t in range(n_kc - 1):
            j = t + jnp.where(t >= blk, 1, 0)
            acc = acc + chunk(j, jnp.where(j > blk, 1, 0), None)
        acc_ref[slot] = acc

    def scores_online(blk, slot):
        signed =(lax.broadcasted_iota(jnp.int32, (tk, 2 * tq), 1) % tq
                  - lax.broadcasted_iota(jnp.int32, (tk, 2 * tq), 0)).astype(F32) * slope

        def body(j, carry):
            mx, l, acc = carry
            off = jnp.asarray(blk * tq - j * tk, F32) * slope
            s = _dot(k_chunk(j), qmain_ref[slot]) - jnp.abs(signed + off)
            m_new = jnp.maximum(mx, jnp.max(s, axis=0, keepdims=True))
            alpha = jnp.exp2(mx - m_new)
            p = jnp.exp2(s - m_new)
            l = alpha * l + jnp.sum(p, axis=0, keepdims=True)
            acc = alpha * acc + _dot(vt_ref[j], p.astype(BF16))
            return m_new, l, acc

        m0 = jnp.full((1, 2 * tq), -jnp.inf, F32)
        l0 = jnp.zeros((1, 2 * tq), F32)
        a0 = jnp.zeros((dv, 2 * tq), F32)
        _, l, acc = lax.fori_loop(0, n_kc, body, (m0, l0, a0))
        acc_ref[slot, 0:dv, :] = acc
        acc_ref[slot, dv:dv + BF16_SUBLANES, :] = jnp.broadcast_to(l, (BF16_SUBLANES, 2 * tq))

    group = BLOCKS_PER_REGION

    def region(first_blk, cur, oth):
        blks = [first_blk + g for g in range(group)]
        prev = [jnp.maximum(b - group, g) for g, b in enumerate(blks)]
        nxt = [jnp.minimum(b + group, n_q - group + g) for g, b in enumerate(blks)]
        fast = fast_ref[cur[0]] == 1
        for s in cur[1:]:
            fast = fast & (fast_ref[s] == 1)

        def body(scores):
            for g in range(group):
                finish(prev[g], oth[g])
            for g in range(group):
                scores(blks[g], cur[g])
            for g in range(group):
                prepare(nxt[g], oth[g])

        pl.when(fast)(lambda: body(scores_fast))
        pl.when(jnp.logical_not(fast))(lambda: body(scores_online))

    slots_a = tuple(range(group))
    slots_b = tuple(range(group, 2 * group))
    for g in range(group):
        acc_ref[slots_b[g]] = jnp.ones(acc_ref.shape[1:], F32)
        prepare(g, slots_a[g])

    def two_regions(u, carry):
        region(2 * group * u, slots_a, slots_b)
        region(2 * group * u + group, slots_b, slots_a)
        return carry

    lax.fori_loop(0, n_q // (2 * group), two_regions, 0)
    for g in range(group):
        finish(n_q - group + g, slots_b[g])


def _attention(slopes, qt, k, vt, lq1, lk1, lq2, lk2, subln_g):
    B, H, n_q, dv, tq = qt.shape
    n_kc = vt.shape[2]
    S = k.shape[2]
    n_slots = 2 * BLOCKS_PER_REGION
    assert n_q % n_slots == 0
    kcol, qrow = _alibi_constants(slopes, S, tq)
    dist = jnp.abs(jnp.arange(TK, dtype=jnp.int32)[:, None]
                   - jnp.arange(tq, dtype=jnp.int32)[None, :]).astype(F32)
    small = lambda n: pl.BlockSpec((1, n), lambda b, h, *_: (0, 0))
    grid_spec = pltpu.PrefetchScalarGridSpec(
        num_scalar_prefetch=1,
        grid=(B, H),
        in_specs=[
            pl.BlockSpec((None, None, n_q, dv, tq), lambda b, h, *_: (b, h, 0, 0, 0)),
            pl.BlockSpec((None, None, S, dv), lambda b, h, *_: (b, h, 0, 0)),
            pl.BlockSpec((None, S, LANES), lambda b, h, *_: (h, 0, 0)),
            pl.BlockSpec((None, BF16_SUBLANES, 2 * tq), lambda b, h, *_: (h, 0, 0)),
            pl.BlockSpec((TK, tq), lambda b, h, *_: (0, 0)),
            pl.BlockSpec((None, None, n_kc, dv, TK), lambda b, h, *_: (b, h, 0, 0, 0)),
            small(DA_HEAD_DIM), small(DA_HEAD_DIM), small(DA_HEAD_DIM), small(DA_HEAD_DIM),
            small(DA_V_DIM),
        ],
        out_specs=pl.BlockSpec((None, S, dv), lambda b, h, *_: (b, 0, h)),
        scratch_shapes=[
            pltpu.VMEM((n_slots, dv, 2 * tq), BF16),
            pltpu.VMEM((n_slots, 3, LANES, 2 * tq), BF16),
            pltpu.VMEM((n_slots, dv + BF16_SUBLANES, 2 * tq), F32),
            pltpu.VMEM((8, LANES), F32),
            pltpu.SMEM((n_slots,), jnp.int32),
        ],
    )
    return pl.pallas_call(
        _attn_kernel,
        grid_spec=grid_spec,
        out_shape=jax.ShapeDtypeStruct((B, S, H * dv), BF16),
        compiler_params=pltpu.CompilerParams(vmem_limit_bytes=VMEM_LIMIT_BYTES),
        name="diff_attention",
    )(slopes, qt, k, kcol, qrow, dist, vt, lq1, lk1, lq2, lk2, subln_g)


def _merge_kernel(x_ref, uv_ref, oda_ref, gate_ref, bg_ref, lng_ref, lnb_ref, sw_ref, sb_ref,
                  psg_ref, pda_ref, wo_ref, h_ref, vn_ref):
    tm = x_ref.shape[0]
    n_chunks = tm // CHUNK

    y_da = _dot(oda_ref[...], pda_ref[...])

    gu = jax.nn.gelu(uv_ref[:, :SG_WIDTH].astype(F32))
    gv = jax.nn.gelu(uv_ref[:, SG_WIDTH:].astype(F32))
    mu = jnp.mean(gv, axis=-1, keepdims=True)
    cen = gv - mu
    var = jnp.mean(cen * cen, axis=-1, keepdims=True)
    vn_ref[...] = (cen * lax.rsqrt(var + EPS) * lng_ref[...] + lnb_ref[...]).astype(BF16)

    lane = lax.broadcasted_iota(jnp.int32, (CHUNK, tm), 1)
    first_group = (lane % LANES) < SG_GROUP_DIM
    mixed = []
    for pr in range(SG_GROUPS // 2):
        vp = jnp.concatenate([vn_ref[c * CHUNK:(c + 1) * CHUNK, pr * LANES:(pr + 1) * LANES]
                              for c in range(n_chunks)], axis=1)
        mixed.append(jnp.where(first_group, _dot(sw_ref[2 * pr], vp), _dot(sw_ref[2 * pr + 1], vp)))
    bias = sb_ref[...]
    sv = jnp.concatenate(
        [jnp.concatenate([m[:, c * LANES:(c + 1) * LANES] for m in mixed], axis=1) + bias
         for c in range(n_chunks)], axis=0)
    o_sg = (gu * sv).astype(BF16)

    y_sg = _dot(o_sg, psg_ref[...])
    g_sg =jax.nn.sigmoid(gate_ref[:, :D_MODEL].astype(F32) + bg_ref[:, :D_MODEL])
    g_da = jax.nn.sigmoid(gate_ref[:, D_MODEL:].astype(F32) + bg_ref[:, D_MODEL:])
    z = (g_sg * y_sg + g_da * y_da).astype(BF16)
    h_ref[...] = x_ref[...] + _dot(z, wo_ref[...])


def _merge(x, uv, o_da, gate_pre, b_gate, ln_g, ln_b, sg_w, sg_bias, p_sg, p_da, w_out):
    B, S, D = x.shape
    tm = TM_MERGE
    tok = lambda width: pl.BlockSpec((None, tm, width), lambda b, i: (b, i, 0))
    return pl.pallas_call(
        _merge_kernel,
        grid=(B, S // tm),
        in_specs=[
            tok(D), tok(2 * SG_WIDTH), tok(DA_WIDTH), tok(2 * D_MODEL),
            _const_spec(b_gate.shape), _const_spec(ln_g.shape), _const_spec(ln_b.shape),
            _const_spec(sg_w.shape), _const_spec(sg_bias.shape),
            _const_spec(p_sg.shape), _const_spec(p_da.shape), _const_spec(w_out.shape),
        ],
        out_specs=tok(D),
        out_shape=jax.ShapeDtypeStruct((B, S, D), F32),
        scratch_shapes=[pltpu.VMEM((tm, SG_WIDTH), BF16)],
        compiler_params=pltpu.CompilerParams(vmem_limit_bytes=VMEM_LIMIT_BYTES),
        name="merge",
    )(x, uv, o_da, gate_pre, b_gate, ln_g, ln_b, sg_w, sg_bias, p_sg, p_da, w_out)


def _ffn_kernel(h_ref, g_ref, wg_ref, wu_ref, wd_ref, o_ref):
    hres = h_ref[...]
    ms = jnp.mean(hres * hres, axis=-1, keepdims=True)
    hn = (hres * lax.rsqrt(ms + EPS) * g_ref[...]).astype(BF16)
    d_ff = wg_ref.shape[1]
    acc = hres
    for c in range(d_ff // FF_CHUNK):
        sl = slice(c * FF_CHUNK, (c + 1) * FF_CHUNK)
        a = jax.nn.silu(_dot(hn, wg_ref[:, sl])) * _dot(hn, wu_ref[:, sl])
        acc = acc + _dot(a.astype(BF16), wd_ref[sl, :])
    o_ref[...] = acc


def _ffn(h, g, w_gate, w_up, w_down):
    B, S, D = h.shape
    tm = TM_FFN
    tok = pl.BlockSpec((None, tm, D), lambda b, i: (b, i, 0))
    return pl.pallas_call(
        _ffn_kernel,
        grid=(B, S // tm),
        in_specs=[tok, _const_spec((1, D)), _const_spec(w_gate.shape), _const_spec(w_up.shape),
                  _const_spec(w_down.shape)],
        out_specs=tok,
        out_shape=jax.ShapeDtypeStruct((B, S, D), F32),
        compiler_params=pltpu.CompilerParams(vmem_limit_bytes=VMEM_LIMIT_BYTES),
        name="swiglu_ffn",
    )(h, g, w_gate, w_up, w_down)


def kernel(x, norm1_g, w_in, b_gate, sg_ln_g, sg_ln_b, sg_w, sg_b, q_norm_g, k_norm_g, lam_q1, lam_k1, lam_q2, lam_k2, subln_g, w_proj_sg, w_proj_da, w_out, norm2_g, w_ffn_gate, w_ffn_up, w_ffn_down):
    depth = w_in.shape[0]
    row = lambda a: a.reshape(1, -1).astype(F32)
    n_rep = DA_WIDTH // DA_HEAD_DIM
    slopes = jnp.asarray([2.0 ** (-8.0 * (i + 1) / DA_HEADS) * LOG2E for i in range(DA_HEADS)], F32)
    for l in range(depth):
        qg = row(jnp.tile(q_norm_g[l], n_rep)) * (LOG2E / math.sqrt(DA_HEAD_DIM))
        kg = row(jnp.tile(k_norm_g[l], n_rep))
        uv, qt, k, vt, gate_pre = _in_projection(x, row(norm1_g[l]), w_in[l].astype(BF16), qg, kg)
        o_da = _attention(slopes, qt, k, vt, row(lam_q1[l]), row(lam_k1[l]), row(lam_q2[l]),
                          row(lam_k2[l]), row(subln_g[l]))
        sg_bias = jnp.repeat(sg_b[l].T.astype(F32), SG_GROUP_DIM, axis=1)
        h = _merge(x, uv, o_da, gate_pre, row(b_gate[l]), row(sg_ln_g[l]), row(sg_ln_b[l]),
                   sg_w[l].astype(BF16), sg_bias,
                   w_proj_sg[l].astype(BF16), w_proj_da[l].astype(BF16), w_out[l].astype(BF16))
        x = _ffn(h, row(norm2_g[l]), w_ffn_gate[l].astype(BF16), w_ffn_up[l].astype(BF16),
                 w_ffn_down[l].astype(BF16))
    return x
```

```python
import math

import jax
import jax.numpy as jnp
from jax import lax
from jax.experimental import pallas as pl
from jax.experimental.pallas import tpu as pltpu

D_MODEL = 1024
SG_GROUPS = 8
SG_GROUP_DIM = 64
SG_WIDTH = SG_GROUPS * SG_GROUP_DIM
CHUNK = 128
DA_HEADS = 8
DA_HEAD_DIM = 64
DA_V_DIM = 2 * DA_HEAD_DIM
DA_WIDTH = DA_HEADS * DA_V_DIM
EPS = 1e-6
LAMBDA_INIT = 0.8 - 0.6 * math.exp(-0.3 * 0)
LOG2E = math.log2(math.e)

BF16 = jnp.bfloat16
F32 = jnp.float32

LANES = 128
BF16_SUBLANES = 16
MXU_DIM = 256
VMEM_LIMIT_BYTES = 56 * 1024 * 1024

TM_PROJ = 512
TQ = 512
TK = 512
POS_SPLIT = 64
N_AUG = 15
SAFE_EXP2_RANGE = 60.0
KMAX_MARGIN = 1.01
BLOCKS_PER_REGION = 2
assert TQ == TK and N_AUG <= BF16_SUBLANES
TM_MERGE = 512
TM_FFN = 1024
FF_CHUNK = 256


def _dot(a, b):
    return jnp.dot(a, b, preferred_element_type=F32)


def _const_spec(shape):
    n = len(shape)
    return pl.BlockSpec(shape, lambda *_: (0,) * n, pipeline_mode=pl.Buffered(1))


def _group_ones(n, value=1.0):
    r = lax.broadcasted_iota(jnp.int32, (n, n), 0) // DA_HEAD_DIM
    c = lax.broadcasted_iota(jnp.int32, (n, n), 1) // DA_HEAD_DIM
    return jnp.where(r == c, value, 0.0).astype(BF16)


def _inproj_kernel(x_ref, n1g_ref, w_ref, qg_ref, kg_ref,
                   uv_ref, qt_ref, k_ref, vt_ref, gate_ref, xn_ref):
    x = x_ref[...]
    ms = jnp.mean(x * x, axis=-1, keepdims=True)
    xn_ref[...] = (x * lax.rsqrt(ms + EPS) * n1g_ref[...]).astype(BF16)
    averager = _group_ones(MXU_DIM, 1.0 / DA_HEAD_DIM)

    def unit_rms(p, gain_ref):
        sq = (p * p).astype(BF16)
        mean_sq = jnp.concatenate([_dot(sq[:, i * MXU_DIM:(i + 1) * MXU_DIM], averager)
                                   for i in range(p.shape[1] // MXU_DIM)], axis=1)
        return p * lax.rsqrt(mean_sq + EPS) * gain_ref[...]

    def spatial(p):
        uv_ref[...] = p.astype(BF16)

    def query(p):
        qn = unit_rms(p, qg_ref)
        for h in range(DA_HEADS):
            qt_ref[h] = qn[:, h * DA_V_DIM:(h + 1) * DA_V_DIM].T.astype(BF16)

    def key(p):
        kn = unit_rms(p, kg_ref).astype(BF16)
        for h in range(DA_HEADS):
            k_ref[h] = kn[:, h * DA_V_DIM:(h + 1) * DA_V_DIM]

    def value(p):
        for h in range(DA_HEADS):
            vt_ref[h] = p[:, h * DA_V_DIM:(h + 1) * DA_V_DIM].T.astype(BF16)

    def gate_sg(p):
        gate_ref[:, :D_MODEL] = p.astype(BF16)

    def gate_da(p):
        gate_ref[:, D_MODEL:] = p.astype(BF16)

    col = 0
    for width, epilogue in ((2 * SG_WIDTH, spatial), (DA_WIDTH, query), (DA_WIDTH, key), (DA_WIDTH, value),
                            (D_MODEL, gate_sg), (D_MODEL, gate_da)):
        epilogue(_dot(xn_ref[...], w_ref[:, col:col + width]))
        col += width


def _in_projection(x, n1g, w_in, qg, kg):
    B, S, D = x.shape
    tm = TM_PROJ
    n_t = S // tm
    n_kc = S // TK
    sub = TK // tm
    in_cols = w_in.shape[1]
    tok = lambda width: pl.BlockSpec((None, tm, width), lambda b, i: (b, i, 0))
    out_shape = (
        jax.ShapeDtypeStruct((B, S, 2 * SG_WIDTH), BF16),
        jax.ShapeDtypeStruct((B, DA_HEADS, S // TQ, DA_V_DIM, TQ), BF16),
        jax.ShapeDtypeStruct((B, DA_HEADS, S, DA_V_DIM), BF16),
        jax.ShapeDtypeStruct((B, DA_HEADS, n_kc, DA_V_DIM, TK), BF16),
        jax.ShapeDtypeStruct((B, S, 2 * D_MODEL), BF16),
    )
    out_specs = (
        tok(2 * SG_WIDTH),
        pl.BlockSpec((None, DA_HEADS, None, DA_V_DIM, tm), lambda b, i: (b, 0, i // sub, 0, i % sub)),
        pl.BlockSpec((None, DA_HEADS, tm, DA_V_DIM), lambda b, i: (b, 0, i, 0)),
        pl.BlockSpec((None, DA_HEADS, None, DA_V_DIM, tm), lambda b, i: (b, 0, i // sub, 0, i % sub)),
        tok(2 * D_MODEL),
    )
    in_specs = [
        tok(D),
        _const_spec((1, D)),
        _const_spec((D, in_cols)),
        _const_spec((1, DA_WIDTH)),
        _const_spec((1, DA_WIDTH)),
    ]
    return pl.pallas_call(
        _inproj_kernel,
        grid=(B, n_t),
        in_specs=in_specs,
        out_specs=out_specs,
        out_shape=out_shape,
        scratch_shapes=[pltpu.VMEM((tm, D), BF16)],
        compiler_params=pltpu.CompilerParams(vmem_limit_bytes=VMEM_LIMIT_BYTES),
        name="in_projection",
    )(x, n1g, w_in, qg, kg)


def _split3(x):
    hi = x.astype(BF16).astype(F32)
    r = x - hi
    mid = r.astype(BF16).astype(F32)
    lo = (r - mid).astype(BF16).astype(F32)
    return hi, mid, lo


def _alibi_constants(slopes, seq, tq):
    parts = jnp.stack(_split3(slopes), axis=0)
    idx = jnp.arange(LANES, dtype=jnp.int32)
    part_of = parts[idx % 3].T
    kpos = jnp.arange(seq, dtype=jnp.int32)[None, :, None]
    a = idx[None, None, :]
    kcol = jnp.where(a < 3, -1.0, jnp.where(a < 9, -part_of[:, None, :],
           jnp.where(a < 12, ((kpos // POS_SPLIT) * POS_SPLIT).astype(F32),
           jnp.where(a < N_AUG, (kpos % POS_SPLIT).astype(F32), 0.0))))
    qloc = (jnp.arange(2 * tq, dtype=jnp.int32) % tq)[None, None, :]
    a = jnp.arange(BF16_SUBLANES, dtype=jnp.int32)[None, :, None]
    qrow = jnp.where((a >= 3) & (a < 6), ((qloc // POS_SPLIT) * POS_SPLIT).astype(F32),
           jnp.where((a >= 6) & (a < 9), (qloc % POS_SPLIT).astype(F32),
           jnp.where((a >= 9) & (a < N_AUG), part_of[:, :BF16_SUBLANES, None], 0.0)))
    return kcol.astype(BF16), qrow


def _attn_kernel(slopes_ref, qt_ref, k_ref, kcol_ref, qrow_ref, dist_ref, vt_ref,
                 lq1_ref, lk1_ref, lq2_ref, lk2_ref, sg_ref, o_ref,
                 qmain_ref, qaug_ref, acc_ref, kmax_ref, fast_ref):
    h = pl.program_id(1)
    n_q, dv, tq = qt_ref.shape
    n_kc, _, tk = vt_ref.shape
    slope = slopes_ref[h]

    def rows_of(index, size):
        start = index * size
        return pl.ds(start if isinstance(start, int) else pl.multiple_of(start, size), size)

    def k_chunk(j):
        return k_ref[rows_of(j, tk), :]

    @pl.when((pl.program_id(0) == 0) & (h == 0))
    def _():
        qaug_ref[:, :, BF16_SUBLANES:, :] = jnp.zeros(
            (qaug_ref.shape[0], 3, LANES - BF16_SUBLANES, 2 * tq), BF16)

    ones_bd = _group_ones(LANES)
    best = jnp.zeros((1, LANES), F32)
    for c in range(n_kc):
        kf = k_ref[c * tk:(c + 1) * tk, :].astype(F32)
        best = jnp.maximum(best, jnp.max(_dot((kf * kf).astype(BF16), ones_bd), axis=0, keepdims=True))
    kmax_ref[0:1, :] = jnp.sqrt(best) * KMAX_MARGIN
    lam = (jnp.exp(jnp.sum(lq1_ref[...] * lk1_ref[...], axis=-1, keepdims=True))
           - jnp.exp(jnp.sum(lq2_ref[...] * lk2_ref[...], axis=-1, keepdims=True)) + LAMBDA_INIT)

    def prepare(blk, slot):
        qt = qt_ref[blk]
        sub = lax.broadcasted_iota(jnp.int32, (dv, tq), 0)
        zero = jnp.zeros_like(qt)
        qmain_ref[slot, :, 0:tq] = jnp.where(sub < DA_HEAD_DIM, qt, zero)
        qmain_ref[slot, :, tq:2 * tq] = jnp.where(sub < DA_HEAD_DIM, zero, qt)
        qsq = qt.astype(F32) * qt.astype(F32)
        n1 = jnp.sqrt(jnp.sum(qsq[:DA_HEAD_DIM], axis=0, keepdims=True))
        n2 = jnp.sqrt(jnp.sum(qsq[DA_HEAD_DIM:], axis=0, keepdims=True))
        m = jnp.concatenate([n1 * kmax_ref[0:1, 0:1], n2 * kmax_ref[0:1, DA_HEAD_DIM:DA_HEAD_DIM + 1]], axis=1)
        a = lax.broadcasted_iota(jnp.int32, (BF16_SUBLANES, 2 * tq), 0)
        positional = qrow_ref[...] + jnp.where((a >= 3) & (a < 6), jnp.asarray(blk * tq, F32), 0.0)
        m_hi, m_mid, m_lo = _split3(m)
        m_rows = jnp.where(a == 0, m_hi, jnp.where(a == 1, m_mid, m_lo))
        for variant, tail in enumerate((positional, -positional, jnp.zeros_like(positional))):
            qaug_ref[slot, variant, 0:BF16_SUBLANES, :] = jnp.where(a < 3, m_rows, tail).astype(BF16)
        fast_ref[slot] = jnp.where(jnp.max(m) <= SAFE_EXP2_RANGE, 1, 0)

    def finish(blk, slot):
        o = acc_ref[slot, 0:dv, :] / acc_ref[slot, dv:dv + 1, :]
        o = o[:, :tq] - lam * o[:, tq:]
        ms = jnp.mean(o * o, axis=0, keepdims=True)
        o = (o * lax.rsqrt(ms + EPS)).T
        o_ref[rows_of(blk, tq), :] = (o * sg_ref[...] * (1.0 - LAMBDA_INIT)).astype(BF16)

    def scores_fast(blk, slot):
        ones_row = jnp.where(lax.broadcasted_iota(jnp.int32, (BF16_SUBLANES, tk), 0) == 0, 1.0, 0.0).astype(BF16)

        def chunk(j, variant, bias):
            k_aug = jnp.concatenate([k_chunk(j), kcol_ref[rows_of(j, tk), :]], axis=1)
            e = _dot(k_aug, jnp.concatenate([qmain_ref[slot], qaug_ref[slot, variant]], axis=0))
            if bias is not None:
                e = e - bias
            p = jnp.exp2(e).astype(BF16)
            return _dot(jnp.concatenate([vt_ref[j], ones_row], axis=0), p)

        diag_bias = dist_ref[...] * slope
        acc = chunk(blk, 2, jnp.concatenate([diag_bias, diag_bias], axis=1))
        for t in range(n_kc - 1):
            j = t + jnp.where(t >= blk, 1, 0)
            acc = acc + chunk(j, jnp.where(j > blk, 1, 0), None)
        acc_ref[slot] = acc

    def scores_online(blk, slot):
        signed =(lax.broadcasted_iota(jnp.int32, (tk, 2 * tq), 1) % tq
                  - lax.broadcasted_iota(jnp.int32, (tk, 2 * tq), 0)).astype(F32) * slope

        def body(j, carry):
            mx, l, acc = carry
            off = jnp.asarray(blk * tq - j * tk, F32) * slope
            s = _dot(k_chunk(j), qmain_ref[slot]) - jnp.abs(signed + off)
            m_new = jnp.maximum(mx, jnp.max(s, axis=0, keepdims=True))
            alpha = jnp.exp2(mx - m_new)
            p = jnp.exp2(s - m_new)
            l = alpha * l + jnp.sum(p, axis=0, keepdims=True)
            acc = alpha * acc + _dot(vt_ref[j], p.astype(BF16))
            return m_new, l, acc

        m0 = jnp.full((1, 2 * tq), -jnp.inf, F32)
        l0 = jnp.zeros((1, 2 * tq), F32)
        a0 = jnp.zeros((dv, 2 * tq), F32)
        _, l, acc = lax.fori_loop(0, n_kc, body, (m0, l0, a0))
        acc_ref[slot, 0:dv, :] = acc
        acc_ref[slot, dv:dv + BF16_SUBLANES, :] = jnp.broadcast_to(l, (BF16_SUBLANES, 2 * tq))

    group = BLOCKS_PER_REGION

    def region(first_blk, cur, oth):
        blks = [first_blk + g for g in range(group)]
        prev = [jnp.maximum(b - group, g) for g, b in enumerate(blks)]
        nxt = [jnp.minimum(b + group, n_q - group + g) for g, b in enumerate(blks)]
        fast = fast_ref[cur[0]] == 1
        for s in cur[1:]:
            fast = fast & (fast_ref[s] == 1)

        def body(scores):
            for g in range(group):
                finish(prev[g], oth[g])
            for g in range(group):
                scores(blks[g], cur[g])
            for g in range(group):
                prepare(nxt[g], oth[g])

        pl.when(fast)(lambda: body(scores_fast))
        pl.when(jnp.logical_not(fast))(lambda: body(scores_online))

    slots_a = tuple(range(group))
    slots_b = tuple(range(group, 2 * group))
    for g in range(group):
        acc_ref[slots_b[g]] = jnp.ones(acc_ref.shape[1:], F32)
        prepare(g, slots_a[g])

    def two_regions(u, carry):
        region(2 * group * u, slots_a, slots_b)
        region(2 * group * u + group, slots_b, slots_a)
        return carry

    lax.fori_loop(0, n_q // (2 * group), two_regions, 0)
    for g in range(group):
        finish(n_q - group + g, slots_b[g])


def _attention(slopes, qt, k, vt, lq1, lk1, lq2, lk2, subln_g):
    B, H, n_q, dv, tq = qt.shape
    n_kc = vt.shape[2]
    S = k.shape[2]
    n_slots = 2 * BLOCKS_PER_REGION
    assert n_q % n_slots == 0
    kcol, qrow = _alibi_constants(slopes, S, tq)
    dist = jnp.abs(jnp.arange(TK, dtype=jnp.int32)[:, None]
                   - jnp.arange(tq, dtype=jnp.int32)[None, :]).astype(F32)
    small = lambda n: pl.BlockSpec((1, n), lambda b, h, *_: (0, 0))
    grid_spec = pltpu.PrefetchScalarGridSpec(
        num_scalar_prefetch=1,
        grid=(B, H),
        in_specs=[
            pl.BlockSpec((None, None, n_q, dv, tq), lambda b, h, *_: (b, h, 0, 0, 0)),
            pl.BlockSpec((None, None, S, dv), lambda b, h, *_: (b, h, 0, 0)),
            pl.BlockSpec((None, S, LANES), lambda b, h, *_: (h, 0, 0)),
            pl.BlockSpec((None, BF16_SUBLANES, 2 * tq), lambda b, h, *_: (h, 0, 0)),
            pl.BlockSpec((TK, tq), lambda b, h, *_: (0, 0)),
            pl.BlockSpec((None, None, n_kc, dv, TK), lambda b, h, *_: (b, h, 0, 0, 0)),
            small(DA_HEAD_DIM), small(DA_HEAD_DIM), small(DA_HEAD_DIM), small(DA_HEAD_DIM),
            small(DA_V_DIM),
        ],
        out_specs=pl.BlockSpec((None, S, dv), lambda b, h, *_: (b, 0, h)),
        scratch_shapes=[
            pltpu.VMEM((n_slots, dv, 2 * tq), BF16),
            pltpu.VMEM((n_slots, 3, LANES, 2 * tq), BF16),
            pltpu.VMEM((n_slots, dv + BF16_SUBLANES, 2 * tq), F32),
            pltpu.VMEM((8, LANES), F32),
            pltpu.SMEM((n_slots,), jnp.int32),
        ],
    )
    return pl.pallas_call(
        _attn_kernel,
        grid_spec=grid_spec,
        out_shape=jax.ShapeDtypeStruct((B, S, H * dv), BF16),
        compiler_params=pltpu.CompilerParams(vmem_limit_bytes=VMEM_LIMIT_BYTES),
        name="diff_attention",
    )(slopes, qt, k, kcol, qrow, dist, vt, lq1, lk1, lq2, lk2, subln_g)


def _merge_tile(x_ref, uv_ref, oda_ref, gate_ref, bg_ref, lng_ref, lnb_ref, sw_ref, sb_ref,
                psg_ref, pda_ref, wo_ref, h_ref, vn_ref):
    tm = x_ref.shape[0]
    n_chunks = tm // CHUNK

    y_da = _dot(oda_ref[...], pda_ref[...])

    gu = jax.nn.gelu(uv_ref[:, :SG_WIDTH].astype(F32))
    gv = jax.nn.gelu(uv_ref[:, SG_WIDTH:].astype(F32))
    mu = jnp.mean(gv, axis=-1, keepdims=True)
    cen = gv - mu
    var = jnp.mean(cen * cen, axis=-1, keepdims=True)
    vn_ref[...] = (cen * lax.rsqrt(var + EPS) * lng_ref[...] + lnb_ref[...]).astype(BF16)

    lane = lax.broadcasted_iota(jnp.int32, (CHUNK, tm), 1)
    first_group = (lane % LANES) < SG_GROUP_DIM
    mixed = []
    for pr in range(SG_GROUPS // 2):
        vp = jnp.concatenate([vn_ref[c * CHUNK:(c + 1) * CHUNK, pr * LANES:(pr + 1) * LANES]
                              for c in range(n_chunks)], axis=1)
        mixed.append(jnp.where(first_group, _dot(sw_ref[2 * pr], vp), _dot(sw_ref[2 * pr + 1], vp)))
    bias = sb_ref[...]
    sv = jnp.concatenate(
        [jnp.concatenate([m[:, c * LANES:(c + 1) * LANES] for m in mixed], axis=1) + bias
         for c in range(n_chunks)], axis=0)
    o_sg = (gu * sv).astype(BF16)

    y_sg = _dot(o_sg, psg_ref[...])
    g_sg =jax.nn.sigmoid(gate_ref[:, :D_MODEL].astype(F32) + bg_ref[:, :D_MODEL])
    g_da = jax.nn.sigmoid(gate_ref[:, D_MODEL:].astype(F32) + bg_ref[:, D_MODEL:])
    z = (g_sg * y_sg + g_da * y_da).astype(BF16)
    h_ref[...] = x_ref[...] + _dot(z, wo_ref[...])


def _ffn_tile(h_ref, g_ref, wg_ref, wu_ref, wd_ref, o_ref):
    hres = h_ref[...]
    ms = jnp.mean(hres * hres, axis=-1, keepdims=True)
    hn = (hres * lax.rsqrt(ms + EPS) * g_ref[...]).astype(BF16)
    d_ff = wg_ref.shape[1]
    acc = hres
    for c in range(d_ff // FF_CHUNK):
        sl = slice(c * FF_CHUNK, (c + 1) * FF_CHUNK)
        a = jax.nn.silu(_dot(hn, wg_ref[:, sl])) * _dot(hn, wu_ref[:, sl])
        acc = acc + _dot(a.astype(BF16), wd_ref[sl, :])
    o_ref[...] = acc


def _merge_ffn_kernel(x_ref, uv_ref, oda_ref, gate_ref, bg_ref, lng_ref, lnb_ref, sw_ref, sb_ref,
                      psg_ref, pda_ref, wo_ref, n2g_ref, wg_ref, wu_ref, wd_ref, o_ref, h_ref, vn_ref):
    s = pl.program_id(1)
    n_tiles = pl.num_programs(1) - 1
    slot = s % 2

    def merge():
        _merge_tile(x_ref, uv_ref, oda_ref, gate_ref, bg_ref, lng_ref, lnb_ref, sw_ref, sb_ref,
                    psg_ref, pda_ref, wo_ref, h_ref.at[slot], vn_ref)

    def ffn():
        _ffn_tile(h_ref.at[1 - slot], n2g_ref, wg_ref, wu_ref, wd_ref, o_ref)

    @pl.when(s == 0)
    def _():
        merge()

    @pl.when((s > 0) & (s < n_tiles))
    def _():
        ffn()
        merge()

    @pl.when(s == n_tiles)
    def _():
        ffn()


def _merge_ffn(x, uv, o_da, gate_pre, b_gate, ln_g, ln_b, sg_w, sg_bias, p_sg, p_da, w_out,
               n2g, w_gate, w_up, w_down):
    B, S, D = x.shape
    tm = TM_MERGE
    n_tiles = S // tm
    merged = lambda width: pl.BlockSpec((None, tm, width), lambda b, s: (b, jnp.minimum(s, n_tiles - 1), 0))
    consts = (b_gate, ln_g, ln_b, sg_w, sg_bias, p_sg, p_da, w_out, n2g, w_gate, w_up, w_down)
    return pl.pallas_call(
        _merge_ffn_kernel,
        grid=(B, n_tiles + 1),
        in_specs=[merged(D), merged(2 * SG_WIDTH), merged(DA_WIDTH), merged(2 * D_MODEL)]
                 + [_const_spec(c.shape) for c in consts],
        out_specs=pl.BlockSpec((None, tm, D), lambda b, s: (b, jnp.maximum(s - 1, 0), 0)),
        out_shape=jax.ShapeDtypeStruct((B, S, D), F32),
        scratch_shapes=[pltpu.VMEM((2, tm, D), F32),
                        pltpu.VMEM((tm, SG_WIDTH), BF16)],
        compiler_params=pltpu.CompilerParams(vmem_limit_bytes=VMEM_LIMIT_BYTES),
        name="merge_ffn",
    )(x, uv, o_da, gate_pre, *consts)


def kernel(x, norm1_g, w_in, b_gate, sg_ln_g, sg_ln_b, sg_w, sg_b, q_norm_g, k_norm_g, lam_q1, lam_k1, lam_q2, lam_k2, subln_g, w_proj_sg, w_proj_da, w_out, norm2_g, w_ffn_gate, w_ffn_up, w_ffn_down):
    depth = w_in.shape[0]
    row = lambda a: a.reshape(1, -1).astype(F32)
    n_rep = DA_WIDTH // DA_HEAD_DIM
    slopes = jnp.asarray([2.0 ** (-8.0 * (i + 1) / DA_HEADS) * LOG2E for i in range(DA_HEADS)], F32)
    for l in range(depth):
        qg = row(jnp.tile(q_norm_g[l], n_rep)) * (LOG2E / math.sqrt(DA_HEAD_DIM))
        kg = row(jnp.tile(k_norm_g[l], n_rep))
        uv, qt, k, vt, gate_pre = _in_projection(x, row(norm1_g[l]), w_in[l].astype(BF16), qg, kg)
        o_da = _attention(slopes, qt, k, vt, row(lam_q1[l]), row(lam_k1[l]), row(lam_q2[l]),
                          row(lam_k2[l]), row(subln_g[l]))
        sg_bias = jnp.repeat(sg_b[l].T.astype(F32), SG_GROUP_DIM, axis=1)
        x = _merge_ffn(x, uv, o_da, gate_pre, row(b_gate[l]), row(sg_ln_g[l]), row(sg_ln_b[l]),
                       sg_w[l].astype(BF16), sg_bias,
                       w_proj_sg[l].astype(BF16), w_proj_da[l].astype(BF16), w_out[l].astype(BF16),
                       row(norm2_g[l]), w_ffn_gate[l].astype(BF16), w_ffn_up[l].astype(BF16),
                       w_ffn_down[l].astype(BF16))
    return x
```

```python
import math

import jax
import jax.numpy as jnp
from jax import lax
from jax.experimental import pallas as pl
from jax.experimental.pallas import tpu as pltpu

D_MODEL = 1024
SG_GROUPS = 8
SG_GROUP_DIM = 64
SG_WIDTH = SG_GROUPS * SG_GROUP_DIM
CHUNK = 128
DA_HEADS = 8
DA_HEAD_DIM = 64
DA_V_DIM = 2 * DA_HEAD_DIM
DA_WIDTH = DA_HEADS * DA_V_DIM
EPS = 1e-6
LAMBDA_INIT = 0.8 - 0.6 * math.exp(-0.3 * 0)
LOG2E = math.log2(math.e)

BF16 = jnp.bfloat16
F32 = jnp.float32

LANES = 128
BF16_SUBLANES = 16
MXU_DIM = 256
VMEM_LIMIT_BYTES = 56 * 1024 * 1024

TM_PROJ = 512
TQ = 512
TK = 512
POS_SPLIT = 64
N_AUG = 15
SAFE_EXP2_RANGE = 60.0
KMAX_MARGIN = 1.01
BLOCKS_PER_REGION = 2
assert TQ == TK and N_AUG <= BF16_SUBLANES
TM_MERGE = 512
TM_FFN = 1024
FF_CHUNK = 256


def _dot(a, b):
    return jnp.dot(a, b, preferred_element_type=F32)


def _const_spec(shape):
    n = len(shape)
    return pl.BlockSpec(shape, lambda *_: (0,) * n, pipeline_mode=pl.Buffered(1))


def _group_ones(n, value=1.0):
    r = lax.broadcasted_iota(jnp.int32, (n, n), 0) // DA_HEAD_DIM
    c = lax.broadcasted_iota(jnp.int32, (n, n), 1) // DA_HEAD_DIM
    return jnp.where(r == c, value, 0.0).astype(BF16)


def _inproj_kernel(x_ref, n1g_ref, w_ref, qg_ref, kg_ref,
                   uv_ref, qt_ref, k_ref, vt_ref, gate_ref, xn_ref):
    x = x_ref[...]
    ms = jnp.mean(x * x, axis=-1, keepdims=True)
    xn_ref[...] = (x * lax.rsqrt(ms + EPS) * n1g_ref[...]).astype(BF16)
    averager = _group_ones(MXU_DIM, 1.0 / DA_HEAD_DIM)

    def unit_rms(p, gain_ref):
        sq = (p * p).astype(BF16)
        mean_sq = jnp.concatenate([_dot(sq[:, i * MXU_DIM:(i + 1) * MXU_DIM], averager)
                                   for i in range(p.shape[1] // MXU_DIM)], axis=1)
        return p * lax.rsqrt(mean_sq + EPS) * gain_ref[...]

    def spatial(p):
        uv_ref[...] = p.astype(BF16)

    def query(p):
        qn = unit_rms(p, qg_ref)
        for h in range(DA_HEADS):
            qt_ref[h] = qn[:, h * DA_V_DIM:(h + 1) * DA_V_DIM].T.astype(BF16)

    def key(p):
        kn = unit_rms(p, kg_ref).astype(BF16)
        for h in range(DA_HEADS):
            k_ref[h] = kn[:, h * DA_V_DIM:(h + 1) * DA_V_DIM]

    def value(p):
        for h in range(DA_HEADS):
            vt_ref[h] = p[:, h * DA_V_DIM:(h + 1) * DA_V_DIM].T.astype(BF16)

    def gate_sg(p):
        gate_ref[:, :D_MODEL] = p.astype(BF16)

    def gate_da(p):
        gate_ref[:, D_MODEL:] = p.astype(BF16)

    col = 0
    for width, epilogue in ((2 * SG_WIDTH, spatial), (DA_WIDTH, query), (DA_WIDTH, key), (DA_WIDTH, value),
                            (D_MODEL, gate_sg), (D_MODEL, gate_da)):
        epilogue(_dot(xn_ref[...], w_ref[:, col:col + width]))
        col += width


def _in_projection(x, n1g, w_in, qg, kg):
    B, S, D = x.shape
    tm = TM_PROJ
    n_t = S // tm
    n_kc = S // TK
    sub = TK // tm
    in_cols = w_in.shape[1]
    tok = lambda width: pl.BlockSpec((None, tm, width), lambda b, i: (b, i, 0))
    out_shape = (
        jax.ShapeDtypeStruct((B, S, 2 * SG_WIDTH), BF16),
        jax.ShapeDtypeStruct((B, DA_HEADS, S // TQ, DA_V_DIM, TQ), BF16),
        jax.ShapeDtypeStruct((B, DA_HEADS, S, DA_V_DIM), BF16),
        jax.ShapeDtypeStruct((B, DA_HEADS, n_kc, DA_V_DIM, TK), BF16),
        jax.ShapeDtypeStruct((B, S, 2 * D_MODEL), BF16),
    )
    out_specs = (
        tok(2 * SG_WIDTH),
        pl.BlockSpec((None, DA_HEADS, None, DA_V_DIM, tm), lambda b, i: (b, 0, i // sub, 0, i % sub)),
        pl.BlockSpec((None, DA_HEADS, tm, DA_V_DIM), lambda b, i: (b, 0, i, 0)),
        pl.BlockSpec((None, DA_HEADS, None, DA_V_DIM, tm), lambda b, i: (b, 0, i // sub, 0, i % sub)),
        tok(2 * D_MODEL),
    )
    in_specs = [
        tok(D),
        _const_spec((1, D)),
        _const_spec((D, in_cols)),
        _const_spec((1, DA_WIDTH)),
        _const_spec((1, DA_WIDTH)),
    ]
    return pl.pallas_call(
        _inproj_kernel,
        grid=(B, n_t),
        in_specs=in_specs,
        out_specs=out_specs,
        out_shape=out_shape,
        scratch_shapes=[pltpu.VMEM((tm, D), BF16)],
        compiler_params=pltpu.CompilerParams(vmem_limit_bytes=VMEM_LIMIT_BYTES),
        name="in_projection",
    )(x, n1g, w_in, qg, kg)


def _split3(x):
    hi = x.astype(BF16).astype(F32)
    r = x - hi
    mid = r.astype(BF16).astype(F32)
    lo = (r - mid).astype(BF16).astype(F32)
    return hi, mid, lo


def _alibi_constants(slopes, seq, tq):
    parts = jnp.stack(_split3(slopes), axis=0)
    idx = jnp.arange(LANES, dtype=jnp.int32)
    part_of = parts[idx % 3].T
    kpos = jnp.arange(seq, dtype=jnp.int32)[None, :, None]
    a = idx[None, None, :]
    kcol = jnp.where(a < 3, -1.0, jnp.where(a < 9, -part_of[:, None, :],
           jnp.where(a < 12, ((kpos // POS_SPLIT) * POS_SPLIT).astype(F32),
           jnp.where(a < N_AUG, (kpos % POS_SPLIT).astype(F32), 0.0))))
    qloc = (jnp.arange(2 * tq, dtype=jnp.int32) % tq)[None, None, :]
    a = jnp.arange(BF16_SUBLANES, dtype=jnp.int32)[None, :, None]
    qrow = jnp.where((a >= 3) & (a < 6), ((qloc // POS_SPLIT) * POS_SPLIT).astype(F32),
           jnp.where((a >= 6) & (a < 9), (qloc % POS_SPLIT).astype(F32),
           jnp.where((a >= 9) & (a < N_AUG), part_of[:, :BF16_SUBLANES, None], 0.0)))
    return kcol.astype(BF16), qrow


def _attn_kernel(slopes_ref, qt_ref, k_ref, kcol_ref, qrow_ref, dist_ref, vt_ref,
                 lq1_ref, lk1_ref, lq2_ref, lk2_ref, sg_ref, o_ref,
                 qmain_ref, qaug_ref, kaug_ref, acc_ref, kmax_ref, fast_ref):
    h = pl.program_id(1)
    n_q, dv, tq = qt_ref.shape
    n_kc, _, tk = vt_ref.shape
    slope = slopes_ref[h]

    def rows_of(index, size):
        start = index * size
        return pl.ds(start if isinstance(start, int) else pl.multiple_of(start, size), size)

    def k_chunk(j):
        return k_ref[rows_of(j, tk), :]

    @pl.when((pl.program_id(0) == 0) & (h == 0))
    def _():
        qaug_ref[:, :, BF16_SUBLANES:, :] = jnp.zeros(
            (qaug_ref.shape[0], 3, LANES - BF16_SUBLANES, 2 * tq), BF16)

    ones_bd = _group_ones(LANES)
    best = jnp.zeros((1, LANES), F32)
    for c in range(n_kc):
        rows = slice(c * tk, (c + 1) * tk)
        kaug_ref[rows, 0:LANES] = k_ref[rows, :]
        kaug_ref[rows, LANES:2 * LANES] = kcol_ref[rows, :]
        kf = k_ref[rows, :].astype(F32)
        best = jnp.maximum(best, jnp.max(_dot((kf * kf).astype(BF16), ones_bd), axis=0, keepdims=True))
    kmax_ref[0:1, :] = jnp.sqrt(best) * KMAX_MARGIN
    lam = (jnp.exp(jnp.sum(lq1_ref[...] * lk1_ref[...], axis=-1, keepdims=True))
           - jnp.exp(jnp.sum(lq2_ref[...] * lk2_ref[...], axis=-1, keepdims=True)) + LAMBDA_INIT)

    def prepare(blk, slot):
        qt = qt_ref[blk]
        sub = lax.broadcasted_iota(jnp.int32, (dv, tq), 0)
        zero = jnp.zeros_like(qt)
        qmain_ref[slot, :, 0:tq] = jnp.where(sub < DA_HEAD_DIM, qt, zero)
        qmain_ref[slot, :, tq:2 * tq] = jnp.where(sub < DA_HEAD_DIM, zero, qt)
        qsq = qt.astype(F32) * qt.astype(F32)
        n1 = jnp.sqrt(jnp.sum(qsq[:DA_HEAD_DIM], axis=0, keepdims=True))
        n2 = jnp.sqrt(jnp.sum(qsq[DA_HEAD_DIM:], axis=0, keepdims=True))
        m = jnp.concatenate([n1 * kmax_ref[0:1, 0:1], n2 * kmax_ref[0:1, DA_HEAD_DIM:DA_HEAD_DIM + 1]], axis=1)
        a = lax.broadcasted_iota(jnp.int32, (BF16_SUBLANES, 2 * tq), 0)
        positional = qrow_ref[...] + jnp.where((a >= 3) & (a < 6), jnp.asarray(blk * tq, F32), 0.0)
        m_hi, m_mid, m_lo = _split3(m)
        m_rows = jnp.where(a == 0, m_hi, jnp.where(a == 1, m_mid, m_lo))
        for variant, tail in enumerate((positional, -positional, jnp.zeros_like(positional))):
            qaug_ref[slot, variant, 0:BF16_SUBLANES, :] = jnp.where(a < 3, m_rows, tail).astype(BF16)
        fast_ref[slot] = jnp.where(jnp.max(m) <= SAFE_EXP2_RANGE, 1, 0)

    def finish(blk, slot):
        o = acc_ref[slot, 0:dv, :] / acc_ref[slot, dv:dv + 1, :]
        o = o[:, :tq] - lam * o[:, tq:]
        ms = jnp.mean(o * o, axis=0, keepdims=True)
        o = (o * lax.rsqrt(ms + EPS)).T
        o_ref[rows_of(blk, tq), :] = (o * sg_ref[...] * (1.0 - LAMBDA_INIT)).astype(BF16)

    def scores_fast(blk, slot):
        ones_row = jnp.where(lax.broadcasted_iota(jnp.int32, (BF16_SUBLANES, tk), 0) == 0, 1.0, 0.0).astype(BF16)

        def chunk(j, variant, bias):
            k_aug = kaug_ref[rows_of(j, tk), :]
            e = _dot(k_aug, jnp.concatenate([qmain_ref[slot], qaug_ref[slot, variant]], axis=0))
            if bias is not None:
                e = e - bias
            p = jnp.exp2(e).astype(BF16)
            return _dot(jnp.concatenate([vt_ref[j], ones_row], axis=0), p)

        diag_bias = dist_ref[...] * slope
        acc = chunk(blk, 2, jnp.concatenate([diag_bias, diag_bias], axis=1))
        for t in range(n_kc - 1):
            j = t + jnp.where(t >= blk, 1, 0)
            acc = acc + chunk(j, jnp.where(j > blk, 1, 0), None)
        acc_ref[slot] = acc

    def scores_online(blk, slot):
        signed =(lax.broadcasted_iota(jnp.int32, (tk, 2 * tq), 1) % tq
                  - lax.broadcasted_iota(jnp.int32, (tk, 2 * tq), 0)).astype(F32) * slope

        def body(j, carry):
            mx, l, acc = carry
            off = jnp.asarray(blk * tq - j * tk, F32) * slope
            s = _dot(k_chunk(j), qmain_ref[slot]) - jnp.abs(signed + off)
            m_new = jnp.maximum(mx, jnp.max(s, axis=0, keepdims=True))
            alpha = jnp.exp2(mx - m_new)
            p = jnp.exp2(s - m_new)
            l = alpha * l + jnp.sum(p, axis=0, keepdims=True)
            acc = alpha * acc + _dot(vt_ref[j], p.astype(BF16))
            return m_new, l, acc

        m0 = jnp.full((1, 2 * tq), -jnp.inf, F32)
        l0 = jnp.zeros((1, 2 * tq), F32)
        a0 = jnp.zeros((dv, 2 * tq), F32)
        _, l, acc = lax.fori_loop(0, n_kc, body, (m0, l0, a0))
        acc_ref[slot, 0:dv, :] = acc
        acc_ref[slot, dv:dv + BF16_SUBLANES, :] = jnp.broadcast_to(l, (BF16_SUBLANES, 2 * tq))

    group = BLOCKS_PER_REGION

    def region(first_blk, cur, oth):
        blks = [first_blk + g for g in range(group)]
        prev = [jnp.maximum(b - group, g) for g, b in enumerate(blks)]
        nxt = [jnp.minimum(b + group, n_q - group + g) for g, b in enumerate(blks)]
        fast = fast_ref[cur[0]] == 1
        for s in cur[1:]:
            fast = fast & (fast_ref[s] == 1)

        def body(scores):
            for g in range(group):
                finish(prev[g], oth[g])
            for g in range(group):
                scores(blks[g], cur[g])
            for g in range(group):
                prepare(nxt[g], oth[g])

        pl.when(fast)(lambda: body(scores_fast))
        pl.when(jnp.logical_not(fast))(lambda: body(scores_online))

    slots_a = tuple(range(group))
    slots_b = tuple(range(group, 2 * group))
    for g in range(group):
        acc_ref[slots_b[g]] = jnp.ones(acc_ref.shape[1:], F32)
        prepare(g, slots_a[g])

    def two_regions(u, carry):
        region(2 * group * u, slots_a, slots_b)
        region(2 * group * u + group, slots_b, slots_a)
        return carry

    lax.fori_loop(0, n_q // (2 * group), two_regions, 0)
    for g in range(group):
        finish(n_q - group + g, slots_b[g])


def _attention(slopes, qt, k, vt, lq1, lk1, lq2, lk2, subln_g):
    B, H, n_q, dv, tq = qt.shape
    n_kc = vt.shape[2]
    S = k.shape[2]
    n_slots = 2 * BLOCKS_PER_REGION
    assert n_q % n_slots == 0
    kcol, qrow = _alibi_constants(slopes, S, tq)
    dist = jnp.abs(jnp.arange(TK, dtype=jnp.int32)[:, None]
                   - jnp.arange(tq, dtype=jnp.int32)[None, :]).astype(F32)
    small = lambda n: pl.BlockSpec((1, n), lambda b, h, *_: (0, 0))
    grid_spec = pltpu.PrefetchScalarGridSpec(
        num_scalar_prefetch=1,
        grid=(B, H),
        in_specs=[
            pl.BlockSpec((None, None, n_q, dv, tq), lambda b, h, *_: (b, h, 0, 0, 0)),
            pl.BlockSpec((None, None, S, dv), lambda b, h, *_: (b, h, 0, 0)),
            pl.BlockSpec((None, S, LANES), lambda b, h, *_: (h, 0, 0)),
            pl.BlockSpec((None, BF16_SUBLANES, 2 * tq), lambda b, h, *_: (h, 0, 0)),
            pl.BlockSpec((TK, tq), lambda b, h, *_: (0, 0)),
            pl.BlockSpec((None, None, n_kc, dv, TK), lambda b, h, *_: (b, h, 0, 0, 0)),
            small(DA_HEAD_DIM), small(DA_HEAD_DIM), small(DA_HEAD_DIM), small(DA_HEAD_DIM),
            small(DA_V_DIM),
        ],
        out_specs=pl.BlockSpec((None, S, dv), lambda b, h, *_: (b, 0, h)),
        scratch_shapes=[
            pltpu.VMEM((n_slots, dv, 2 * tq), BF16),
            pltpu.VMEM((n_slots, 3, LANES, 2 * tq), BF16),
            pltpu.VMEM((S, dv + LANES), BF16),
            pltpu.VMEM((n_slots, dv + BF16_SUBLANES, 2 * tq), F32),
            pltpu.VMEM((8, LANES), F32),
            pltpu.SMEM((n_slots,), jnp.int32),
        ],
    )
    return pl.pallas_call(
        _attn_kernel,
        grid_spec=grid_spec,
        out_shape=jax.ShapeDtypeStruct((B, S, H * dv), BF16),
        compiler_params=pltpu.CompilerParams(vmem_limit_bytes=VMEM_LIMIT_BYTES),
        name="diff_attention",
    )(slopes, qt, k, kcol, qrow, dist, vt, lq1, lk1, lq2, lk2, subln_g)


def _merge_kernel(x_ref, uv_ref, oda_ref, gate_ref, bg_ref, lng_ref, lnb_ref, sw_ref, sb_ref,
                  psg_ref, pda_ref, wo_ref, h_ref, vn_ref):
    tm = x_ref.shape[0]
    n_chunks = tm // CHUNK

    y_da = _dot(oda_ref[...], pda_ref[...])

    gu = jax.nn.gelu(uv_ref[:, :SG_WIDTH].astype(F32))
    gv = jax.nn.gelu(uv_ref[:, SG_WIDTH:].astype(F32))
    mu = jnp.mean(gv, axis=-1, keepdims=True)
    cen = gv - mu
    var = jnp.mean(cen * cen, axis=-1, keepdims=True)
    vn_ref[...] = (cen * lax.rsqrt(var + EPS) * lng_ref[...] + lnb_ref[...]).astype(BF16)

    lane = lax.broadcasted_iota(jnp.int32, (CHUNK, tm), 1)
    first_group = (lane % LANES) < SG_GROUP_DIM
    mixed = []
    for pr in range(SG_GROUPS // 2):
        vp = jnp.concatenate([vn_ref[c * CHUNK:(c + 1) * CHUNK, pr * LANES:(pr + 1) * LANES]
                              for c in range(n_chunks)], axis=1)
        mixed.append(jnp.where(first_group, _dot(sw_ref[2 * pr], vp), _dot(sw_ref[2 * pr + 1], vp)))
    bias = sb_ref[...]
    sv = jnp.concatenate(
        [jnp.concatenate([m[:, c * LANES:(c + 1) * LANES] for m in mixed], axis=1) + bias
         for c in range(n_chunks)], axis=0)
    o_sg = (gu * sv).astype(BF16)

    y_sg = _dot(o_sg, psg_ref[...])
    g_sg =jax.nn.sigmoid(gate_ref[:, :D_MODEL].astype(F32) + bg_ref[:, :D_MODEL])
    g_da = jax.nn.sigmoid(gate_ref[:, D_MODEL:].astype(F32) + bg_ref[:, D_MODEL:])
    z = (g_sg * y_sg + g_da * y_da).astype(BF16)
    h_ref[...] = x_ref[...] + _dot(z, wo_ref[...])


def _merge(x, uv, o_da, gate_pre, b_gate, ln_g, ln_b, sg_w, sg_bias, p_sg, p_da, w_out):
    B, S, D = x.shape
    tm = TM_MERGE
    tok = lambda width: pl.BlockSpec((None, tm, width), lambda b, i: (b, i, 0))
    return pl.pallas_call(
        _merge_kernel,
        grid=(B, S // tm),
        in_specs=[
            tok(D), tok(2 * SG_WIDTH), tok(DA_WIDTH), tok(2 * D_MODEL),
            _const_spec(b_gate.shape), _const_spec(ln_g.shape), _const_spec(ln_b.shape),
            _const_spec(sg_w.shape), _const_spec(sg_bias.shape),
            _const_spec(p_sg.shape), _const_spec(p_da.shape), _const_spec(w_out.shape),
        ],
        out_specs=tok(D),
        out_shape=jax.ShapeDtypeStruct((B, S, D), F32),
        scratch_shapes=[pltpu.VMEM((tm, SG_WIDTH), BF16)],
        compiler_params=pltpu.CompilerParams(vmem_limit_bytes=VMEM_LIMIT_BYTES),
        name="merge",
    )(x, uv, o_da, gate_pre, b_gate, ln_g, ln_b, sg_w, sg_bias, p_sg, p_da, w_out)


def _ffn_kernel(h_ref, g_ref, wg_ref, wu_ref, wd_ref, o_ref):
    hres = h_ref[...]
    ms = jnp.mean(hres * hres, axis=-1, keepdims=True)
    hn = (hres * lax.rsqrt(ms + EPS) * g_ref[...]).astype(BF16)
    d_ff = wg_ref.shape[1]
    acc = hres
    for c in range(d_ff // FF_CHUNK):
        sl = slice(c * FF_CHUNK, (c + 1) * FF_CHUNK)
        a = jax.nn.silu(_dot(hn, wg_ref[:, sl])) * _dot(hn, wu_ref[:, sl])
        acc = acc + _dot(a.astype(BF16), wd_ref[sl, :])
    o_ref[...] = acc


def _ffn(h, g, w_gate, w_up, w_down):
    B, S, D = h.shape
    tm = TM_FFN
    tok = pl.BlockSpec((None, tm, D), lambda b, i: (b, i, 0))
    return pl.pallas_call(
        _ffn_kernel,
        grid=(B, S // tm),
        in_specs=[tok, _const_spec((1, D)), _const_spec(w_gate.shape), _const_spec(w_up.shape),
                  _const_spec(w_down.shape)],
        out_specs=tok,
        out_shape=jax.ShapeDtypeStruct((B, S, D), F32),
        compiler_params=pltpu.CompilerParams(vmem_limit_bytes=VMEM_LIMIT_BYTES),
        name="swiglu_ffn",
    )(h, g, w_gate, w_up, w_down)


def kernel(x, norm1_g, w_in, b_gate, sg_ln_g, sg_ln_b, sg_w, sg_b, q_norm_g, k_norm_g, lam_q1, lam_k1, lam_q2, lam_k2, subln_g, w_proj_sg, w_proj_da, w_out, norm2_g, w_ffn_gate, w_ffn_up, w_ffn_down):
    depth = w_in.shape[0]
    row = lambda a: a.reshape(1, -1).astype(F32)
    n_rep = DA_WIDTH // DA_HEAD_DIM
    slopes = jnp.asarray([2.0 ** (-8.0 * (i + 1) / DA_HEADS) * LOG2E for i in range(DA_HEADS)], F32)
    for l in range(depth):
        qg = row(jnp.tile(q_norm_g[l], n_rep)) * (LOG2E / math.sqrt(DA_HEAD_DIM))
        kg = row(jnp.tile(k_norm_g[l], n_rep))
        uv, qt, k, vt, gate_pre = _in_projection(x, row(norm1_g[l]), w_in[l].astype(BF16), qg, kg)
        o_da = _attention(slopes, qt, k, vt, row(lam_q1[l]), row(lam_k1[l]), row(lam_q2[l]),
                          row(lam_k2[l]), row(subln_g[l]))
        sg_bias = jnp.repeat(sg_b[l].T.astype(F32), SG_GROUP_DIM, axis=1)
        h = _merge(x, uv, o_da, gate_pre, row(b_gate[l]), row(sg_ln_g[l]), row(sg_ln_b[l]),
                   sg_w[l].astype(BF16), sg_bias,
                   w_proj_sg[l].astype(BF16), w_proj_da[l].astype(BF16), w_out[l].astype(BF16))
        x = _ffn(h, row(norm2_g[l]), w_ffn_gate[l].astype(BF16), w_ffn_up[l].astype(BF16),
                 w_ffn_down[l].astype(BF16))
    return x
```

```python
import math

import jax
import jax.numpy as jnp
from jax import lax
from jax.experimental import pallas as pl
from jax.experimental.pallas import tpu as pltpu

D_MODEL = 1024
SG_GROUPS = 8
SG_GROUP_DIM = 64
SG_WIDTH = SG_GROUPS * SG_GROUP_DIM
CHUNK = 128
DA_HEADS = 8
DA_HEAD_DIM = 64
DA_V_DIM = 2 * DA_HEAD_DIM
DA_WIDTH = DA_HEADS * DA_V_DIM
EPS = 1e-6
LAMBDA_INIT = 0.8 - 0.6 * math.exp(-0.3 * 0)
LOG2E = math.log2(math.e)

BF16 = jnp.bfloat16
F32 = jnp.float32

LANES = 128
BF16_SUBLANES = 16
MXU_DIM = 256
VMEM_LIMIT_BYTES = 56 * 1024 * 1024

TM_PROJ = 512
TQ = 512
TK = 512
POS_SPLIT = 64
N_AUG = 15
SAFE_EXP2_RANGE = 60.0
KMAX_MARGIN = 1.01
BLOCKS_PER_REGION = 2
assert TQ == TK and N_AUG <= BF16_SUBLANES
TM_MERGE = 512
TM_FFN = 1024
FF_CHUNK = 256


def _dot(a, b):
    return jnp.dot(a, b, preferred_element_type=F32)


def _const_spec(shape):
    n = len(shape)
    return pl.BlockSpec(shape, lambda *_: (0,) * n, pipeline_mode=pl.Buffered(1))


def _group_ones(n, value=1.0):
    r = lax.broadcasted_iota(jnp.int32, (n, n), 0) // DA_HEAD_DIM
    c = lax.broadcasted_iota(jnp.int32, (n, n), 1) // DA_HEAD_DIM
    return jnp.where(r == c, value, 0.0).astype(BF16)


def _inproj_kernel(x_ref, n1g_ref, w_ref, qg_ref, kg_ref,
                   uv_ref, qt_ref, k_ref, vt_ref, gate_ref, xn_ref):
    x = x_ref[...]
    ms = jnp.mean(x * x, axis=-1, keepdims=True)
    xn_ref[...] = (x * lax.rsqrt(ms + EPS) * n1g_ref[...]).astype(BF16)
    averager = _group_ones(MXU_DIM, 1.0 / DA_HEAD_DIM)

    def unit_rms(p, gain_ref):
        sq = (p * p).astype(BF16)
        mean_sq = jnp.concatenate([_dot(sq[:, i * MXU_DIM:(i + 1) * MXU_DIM], averager)
                                   for i in range(p.shape[1] // MXU_DIM)], axis=1)
        return p * lax.rsqrt(mean_sq + EPS) * gain_ref[...]

    def spatial(p):
        uv_ref[...] = p.astype(BF16)

    def query(p):
        qn = unit_rms(p, qg_ref)
        for h in range(DA_HEADS):
            qt_ref[h] = qn[:, h * DA_V_DIM:(h + 1) * DA_V_DIM].T.astype(BF16)

    def key(p):
        kn = unit_rms(p, kg_ref).astype(BF16)
        for h in range(DA_HEADS):
            k_ref[h] = kn[:, h * DA_V_DIM:(h + 1) * DA_V_DIM]

    def value(p):
        for h in range(DA_HEADS):
            vt_ref[h] = p[:, h * DA_V_DIM:(h + 1) * DA_V_DIM].T.astype(BF16)

    def gate_sg(p):
        gate_ref[:, :D_MODEL] = p.astype(BF16)

    def gate_da(p):
        gate_ref[:, D_MODEL:] = p.astype(BF16)

    col = 0
    for width, epilogue in ((2 * SG_WIDTH, spatial), (DA_WIDTH, query), (DA_WIDTH, key), (DA_WIDTH, value),
                            (D_MODEL, gate_sg), (D_MODEL, gate_da)):
        epilogue(_dot(xn_ref[...], w_ref[:, col:col + width]))
        col += width


def _in_projection(x, n1g, w_in, qg, kg):
    B, S, D = x.shape
    tm = TM_PROJ
    n_t = S // tm
    n_kc = S // TK
    sub = TK // tm
    in_cols = w_in.shape[1]
    tok = lambda width: pl.BlockSpec((None, tm, width), lambda b, i: (b, i, 0))
    out_shape = (
        jax.ShapeDtypeStruct((B, S, 2 * SG_WIDTH), BF16),
        jax.ShapeDtypeStruct((B, DA_HEADS, S // TQ, DA_V_DIM, TQ), BF16),
        jax.ShapeDtypeStruct((B, DA_HEADS, S, DA_V_DIM), BF16),
        jax.ShapeDtypeStruct((B, DA_HEADS, n_kc, DA_V_DIM, TK), BF16),
        jax.ShapeDtypeStruct((B, S, 2 * D_MODEL), BF16),
    )
    out_specs = (
        tok(2 * SG_WIDTH),
        pl.BlockSpec((None, DA_HEADS, None, DA_V_DIM, tm), lambda b, i: (b, 0, i // sub, 0, i % sub)),
        pl.BlockSpec((None, DA_HEADS, tm, DA_V_DIM), lambda b, i: (b, 0, i, 0)),
        pl.BlockSpec((None, DA_HEADS, None, DA_V_DIM, tm), lambda b, i: (b, 0, i // sub, 0, i % sub)),
        tok(2 * D_MODEL),
    )
    in_specs = [
        tok(D),
        _const_spec((1, D)),
        _const_spec((D, in_cols)),
        _const_spec((1, DA_WIDTH)),
        _const_spec((1, DA_WIDTH)),
    ]
    return pl.pallas_call(
        _inproj_kernel,
        grid=(B, n_t),
        in_specs=in_specs,
        out_specs=out_specs,
        out_shape=out_shape,
        scratch_shapes=[pltpu.VMEM((tm, D), BF16)],
        compiler_params=pltpu.CompilerParams(vmem_limit_bytes=VMEM_LIMIT_BYTES),
        name="in_projection",
    )(x, n1g, w_in, qg, kg)


def _split3(x):
    hi = x.astype(BF16).astype(F32)
    r = x - hi
    mid = r.astype(BF16).astype(F32)
    lo = (r - mid).astype(BF16).astype(F32)
    return hi, mid, lo


def _alibi_constants(slopes, seq, tq):
    parts = jnp.stack(_split3(slopes), axis=0)
    idx = jnp.arange(LANES, dtype=jnp.int32)
    part_of = parts[idx % 3].T
    kpos = jnp.arange(seq, dtype=jnp.int32)[None, :, None]
    a = idx[None, None, :]
    kcol = jnp.where(a < 3, -1.0, jnp.where(a < 9, -part_of[:, None, :],
           jnp.where(a < 12, ((kpos // POS_SPLIT) * POS_SPLIT).astype(F32),
           jnp.where(a < N_AUG, (kpos % POS_SPLIT).astype(F32), 0.0))))
    qloc = (jnp.arange(2 * tq, dtype=jnp.int32) % tq)[None, None, :]
    a = jnp.arange(BF16_SUBLANES, dtype=jnp.int32)[None, :, None]
    qrow = jnp.where((a >= 3) & (a < 6), ((qloc // POS_SPLIT) * POS_SPLIT).astype(F32),
           jnp.where((a >= 6) & (a < 9), (qloc % POS_SPLIT).astype(F32),
           jnp.where((a >= 9) & (a < N_AUG), part_of[:, :BF16_SUBLANES, None], 0.0)))
    return kcol.astype(BF16), qrow


def _attn_kernel(slopes_ref, qt_ref, k_ref, kcol_ref, qrow_ref, dist_ref, vt_ref,
                 lq1_ref, lk1_ref, lq2_ref, lk2_ref, sg_ref, o_ref,
                 qmain_ref, qaug_ref, acc_ref, kmax_ref, fast_ref):
    h = pl.program_id(1)
    n_q, dv, tq = qt_ref.shape
    n_kc, _, tk = vt_ref.shape
    slope = slopes_ref[h]

    def rows_of(index, size):
        start = index * size
        return pl.ds(start if isinstance(start, int) else pl.multiple_of(start, size), size)

    def k_chunk(j):
        return k_ref[rows_of(j, tk), :]

    @pl.when((pl.program_id(0) == 0) & (h == 0))
    def _():
        qaug_ref[:, :, BF16_SUBLANES:, :] = jnp.zeros(
            (qaug_ref.shape[0], 3, LANES - BF16_SUBLANES, 2 * tq), BF16)

    ones_bd = _group_ones(LANES)
    best = jnp.zeros((1, LANES), F32)
    for c in range(n_kc):
        kf = k_ref[c * tk:(c + 1) * tk, :].astype(F32)
        best = jnp.maximum(best, jnp.max(_dot((kf * kf).astype(BF16), ones_bd), axis=0, keepdims=True))
    kmax_ref[0:1, :] = jnp.sqrt(best) * KMAX_MARGIN
    lam = (jnp.exp(jnp.sum(lq1_ref[...] * lk1_ref[...], axis=-1, keepdims=True))
           - jnp.exp(jnp.sum(lq2_ref[...] * lk2_ref[...], axis=-1, keepdims=True)) + LAMBDA_INIT)

    def prepare(blk, slot):
        qt = qt_ref[blk]
        sub = lax.broadcasted_iota(jnp.int32, (dv, tq), 0)
        zero = jnp.zeros_like(qt)
        qmain_ref[slot, :, 0:tq] = jnp.where(sub < DA_HEAD_DIM, qt, zero)
        qmain_ref[slot, :, tq:2 * tq] = jnp.where(sub < DA_HEAD_DIM, zero, qt)
        qsq = qt.astype(F32) * qt.astype(F32)
        n1 = jnp.sqrt(jnp.sum(qsq[:DA_HEAD_DIM], axis=0, keepdims=True))
        n2 = jnp.sqrt(jnp.sum(qsq[DA_HEAD_DIM:], axis=0, keepdims=True))
        m = jnp.concatenate([n1 * kmax_ref[0:1, 0:1], n2 * kmax_ref[0:1, DA_HEAD_DIM:DA_HEAD_DIM + 1]], axis=1)
        a = lax.broadcasted_iota(jnp.int32, (BF16_SUBLANES, 2 * tq), 0)
        positional = qrow_ref[...] + jnp.where((a >= 3) & (a < 6), jnp.asarray(blk * tq, F32), 0.0)
        m_hi, m_mid, m_lo = _split3(m)
        m_rows = jnp.where(a == 0, m_hi, jnp.where(a == 1, m_mid, m_lo))
        for variant, tail in enumerate((positional, -positional, jnp.zeros_like(positional))):
            qaug_ref[slot, variant, 0:BF16_SUBLANES, :] = jnp.where(a < 3, m_rows, tail).astype(BF16)
        fast_ref[slot] = jnp.where(jnp.max(m) <= SAFE_EXP2_RANGE, 1, 0)

    def finish(blk, slot):
        o = acc_ref[slot, 0:dv, :] / acc_ref[slot, dv:dv + 1, :]
        o = o[:, :tq] - lam * o[:, tq:]
        ms = jnp.mean(o * o, axis=0, keepdims=True)
        o = (o * lax.rsqrt(ms + EPS)).T
        o_ref[rows_of(blk, tq), :] = (o * sg_ref[...] * (1.0 - LAMBDA_INIT)).astype(BF16)

    def scores_fast(blk, slot):
        ones_row = jnp.where(lax.broadcasted_iota(jnp.int32, (BF16_SUBLANES, tk), 0) == 0, 1.0, 0.0).astype(BF16)

        def chunk(j, variant, bias):
            k_aug = jnp.concatenate([k_chunk(j), kcol_ref[rows_of(j, tk), :]], axis=1)
            e = _dot(k_aug, jnp.concatenate([qmain_ref[slot], qaug_ref[slot, variant]], axis=0))
            if bias is not None:
                e = e - bias
            p = jnp.exp2(e).astype(BF16)
            return _dot(jnp.concatenate([vt_ref[j], ones_row], axis=0), p)

        diag_bias = dist_ref[...] * slope
        acc = chunk(blk, 2, jnp.concatenate([diag_bias, diag_bias], axis=1))
        for t in range(n_kc - 1):
            j = t + jnp.where(t >= blk, 1, 0)
            acc = acc + chunk(j, jnp.where(j > blk, 1, 0), None)
        acc_ref[slot] = acc

    def scores_online(blk, slot):
        signed =(lax.broadcasted_iota(jnp.int32, (tk, 2 * tq), 1) % tq
                  - lax.broadcasted_iota(jnp.int32, (tk, 2 * tq), 0)).astype(F32) * slope

        def body(j, carry):
            mx, l, acc = carry
            off = jnp.asarray(blk * tq - j * tk, F32) * slope
            s = _dot(k_chunk(j), qmain_ref[slot]) - jnp.abs(signed + off)
            m_new = jnp.maximum(mx, jnp.max(s, axis=0, keepdims=True))
            alpha = jnp.exp2(mx - m_new)
            p = jnp.exp2(s - m_new)
            l = alpha * l + jnp.sum(p, axis=0, keepdims=True)
            acc = alpha * acc + _dot(vt_ref[j], p.astype(BF16))
            return m_new, l, acc

        m0 = jnp.full((1, 2 * tq), -jnp.inf, F32)
        l0 = jnp.zeros((1, 2 * tq), F32)
        a0 = jnp.zeros((dv, 2 * tq), F32)
        _, l, acc = lax.fori_loop(0, n_kc, body, (m0, l0, a0))
        acc_ref[slot, 0:dv, :] = acc
        acc_ref[slot, dv:dv + BF16_SUBLANES, :] = jnp.broadcast_to(l, (BF16_SUBLANES, 2 * tq))

    group = BLOCKS_PER_REGION

    def region(first_blk, cur, oth):
        blks = [first_blk + g for g in range(group)]
        prev = [jnp.maximum(b - group, g) for g, b in enumerate(blks)]
        nxt = [jnp.minimum(b + group, n_q - group + g) for g, b in enumerate(blks)]
        fast = fast_ref[cur[0]] == 1
        for s in cur[1:]:
            fast = fast & (fast_ref[s] == 1)

        def body(scores):
            for g in range(group):
                finish(prev[g], oth[g])
            for g in range(group):
                scores(blks[g], cur[g])
            for g in range(group):
                prepare(nxt[g], oth[g])

        pl.when(fast)(lambda: body(scores_fast))
        pl.when(jnp.logical_not(fast))(lambda: body(scores_online))

    slots_a = tuple(range(group))
    slots_b = tuple(range(group, 2 * group))
    for g in range(group):
        acc_ref[slots_b[g]] = jnp.ones(acc_ref.shape[1:], F32)
        prepare(g, slots_a[g])

    def two_regions(u, carry):
        region(2 * group * u, slots_a, slots_b)
        region(2 * group * u + group, slots_b, slots_a)
        return carry

    lax.fori_loop(0, n_q // (2 * group), two_regions, 0)
    for g in range(group):
        finish(n_q - group + g, slots_b[g])


def _attention(slopes, qt, k, vt, lq1, lk1, lq2, lk2, subln_g):
    B, H, n_q, dv, tq = qt.shape
    n_kc = vt.shape[2]
    S = k.shape[2]
    n_slots = 2 * BLOCKS_PER_REGION
    assert n_q % n_slots == 0
    kcol, qrow = _alibi_constants(slopes, S, tq)
    dist = jnp.abs(jnp.arange(TK, dtype=jnp.int32)[:, None]
                   - jnp.arange(tq, dtype=jnp.int32)[None, :]).astype(F32)
    small = lambda n: pl.BlockSpec((1, n), lambda b, h, *_: (0, 0))
    grid_spec = pltpu.PrefetchScalarGridSpec(
        num_scalar_prefetch=1,
        grid=(B, H),
        in_specs=[
            pl.BlockSpec((None, None, n_q, dv, tq), lambda b, h, *_: (b, h, 0, 0, 0)),
            pl.BlockSpec((None, None, S, dv), lambda b, h, *_: (b, h, 0, 0)),
            pl.BlockSpec((None, S, LANES), lambda b, h, *_: (h, 0, 0)),
            pl.BlockSpec((None, BF16_SUBLANES, 2 * tq), lambda b, h, *_: (h, 0, 0)),
            pl.BlockSpec((TK, tq), lambda b, h, *_: (0, 0)),
            pl.BlockSpec((None, None, n_kc, dv, TK), lambda b, h, *_: (b, h, 0, 0, 0)),
            small(DA_HEAD_DIM), small(DA_HEAD_DIM), small(DA_HEAD_DIM), small(DA_HEAD_DIM),
            small(DA_V_DIM),
        ],
        out_specs=pl.BlockSpec((None, S, dv), lambda b, h, *_: (b, 0, h)),
        scratch_shapes=[
            pltpu.VMEM((n_slots, dv, 2 * tq), BF16),
            pltpu.VMEM((n_slots, 3, LANES, 2 * tq), BF16),
            pltpu.VMEM((n_slots, dv + BF16_SUBLANES, 2 * tq), F32),
            pltpu.VMEM((8, LANES), F32),
            pltpu.SMEM((n_slots,), jnp.int32),
        ],
    )
    return pl.pallas_call(
        _attn_kernel,
        grid_spec=grid_spec,
        out_shape=jax.ShapeDtypeStruct((B, S, H * dv), BF16),
        compiler_params=pltpu.CompilerParams(vmem_limit_bytes=VMEM_LIMIT_BYTES),
        name="diff_attention",
    )(slopes, qt, k, kcol, qrow, dist, vt, lq1, lk1, lq2, lk2, subln_g)


def _merge_kernel(x_ref, uv_ref, oda_ref, gate_ref, bg_ref, lng_ref, lnb_ref, sw_ref, sb_ref,
                  psg_ref, pda_ref, wo_ref, h_ref, vn_ref):
    tm = x_ref.shape[0]
    n_chunks = tm // CHUNK

    y_da = _dot(oda_ref[...], pda_ref[...])

    gu = jax.nn.gelu(uv_ref[:, :SG_WIDTH].astype(F32))
    gv = jax.nn.gelu(uv_ref[:, SG_WIDTH:].astype(F32))
    mu = jnp.mean(gv, axis=-1, keepdims=True)
    cen = gv - mu
    var = jnp.mean(cen * cen, axis=-1, keepdims=True)
    vn_ref[...] = (cen * lax.rsqrt(var + EPS) * lng_ref[...] + lnb_ref[...]).astype(BF16)

    lane = lax.broadcasted_iota(jnp.int32, (CHUNK, tm), 1)
    first_group = (lane % LANES) < SG_GROUP_DIM
    mixed = []
    for pr in range(SG_GROUPS // 2):
        vp = jnp.concatenate([vn_ref[c * CHUNK:(c + 1) * CHUNK, pr * LANES:(pr + 1) * LANES]
                              for c in range(n_chunks)], axis=1)
        mixed.append(jnp.where(first_group, _dot(sw_ref[2 * pr], vp), _dot(sw_ref[2 * pr + 1], vp)))
    bias = sb_ref[...]
    sv = jnp.concatenate(
        [jnp.concatenate([m[:, c * LANES:(c + 1) * LANES] for m in mixed], axis=1) + bias
         for c in range(n_chunks)], axis=0)
    o_sg = (gu * sv).astype(BF16)

    y_sg = _dot(o_sg, psg_ref[...])
    g_sg = jax.nn.sigmoid(gate_ref[:, :D_MODEL].astype(F32) + bg_ref[:, :D_MODEL])
    g_da = jax.nn.sigmoid(gate_ref[:, D_MODEL:].astype(F32) + bg_ref[:, D_MODEL:])
    z = (g_sg * y_sg + g_da * y_da).astype(BF16)
    h_ref[...] = x_ref[...] + _dot(z, wo_ref[...])


def _merge(x, uv, o_da, gate_pre, b_gate, ln_g, ln_b, sg_w, sg_bias, p_sg, p_da, w_out):
    B, S, D = x.shape
    tm = TM_MERGE
    tok = lambda width: pl.BlockSpec((None, tm, width), lambda b, i: (b, i, 0))
    return pl.pallas_call(
        _merge_kernel,
        grid=(B, S // tm),
        in_specs=[
            tok(D), tok(2 * SG_WIDTH), tok(DA_WIDTH), tok(2 * D_MODEL),
            _const_spec(b_gate.shape), _const_spec(ln_g.shape), _const_spec(ln_b.shape),
            _const_spec(sg_w.shape), _const_spec(sg_bias.shape),
            _const_spec(p_sg.shape), _const_spec(p_da.shape), _const_spec(w_out.shape),
        ],
        out_specs=tok(D),
        out_shape=jax.ShapeDtypeStruct((B, S, D), F32),
        scratch_shapes=[pltpu.VMEM((tm, SG_WIDTH), BF16)],
        compiler_params=pltpu.CompilerParams(vmem_limit_bytes=VMEM_LIMIT_BYTES),
        name="merge",
    )(x, uv, o_da, gate_pre, b_gate, ln_g, ln_b, sg_w, sg_bias, p_sg, p_da, w_out)


def _ffn_kernel(h_ref, g_ref, wg_ref, wu_ref, wd_ref, o_ref):
    hres = h_ref[...]
    ms = jnp.mean(hres * hres, axis=-1, keepdims=True)
    hn = (hres * lax.rsqrt(ms + EPS) * g_ref[...]).astype(BF16)
    d_ff = wg_ref.shape[1]
    acc = hres
    for c in range(d_ff // FF_CHUNK):
        sl = slice(c * FF_CHUNK, (c + 1) * FF_CHUNK)
        a = jax.nn.silu(_dot(hn, wg_ref[:, sl])) * _dot(hn, wu_ref[:, sl])
        acc = acc + _dot(a.astype(BF16), wd_ref[sl, :])
    o_ref[...] = acc


def _ffn(h, g, w_gate, w_up, w_down):
    B, S, D = h.shape
    tm = TM_FFN
    tok = pl.BlockSpec((None, tm, D), lambda b, i: (b, i, 0))
    return pl.pallas_call(
        _ffn_kernel,
        grid=(B, S // tm),
        in_specs=[tok, _const_spec((1, D)), _const_spec(w_gate.shape), _const_spec(w_up.shape),
                  _const_spec(w_down.shape)],
        out_specs=tok,
        out_shape=jax.ShapeDtypeStruct((B, S, D), F32),
        compiler_params=pltpu.CompilerParams(vmem_limit_bytes=VMEM_LIMIT_BYTES),
        name="swiglu_ffn",
    )(h, g, w_gate, w_up, w_down)


def kernel(x, norm1_g, w_in, b_gate, sg_ln_g, sg_ln_b, sg_w, sg_b, q_norm_g, k_norm_g, lam_q1, lam_k1, lam_q2, lam_k2, subln_g, w_proj_sg, w_proj_da, w_out, norm2_g, w_ffn_gate, w_ffn_up, w_ffn_down):
    depth = w_in.shape[0]
    row = lambda a: a.reshape(1, -1).astype(F32)
    n_rep = DA_WIDTH // DA_HEAD_DIM
    slopes = jnp.asarray([2.0 ** (-8.0 * (i + 1) / DA_HEADS) * LOG2E for i in range(DA_HEADS)], F32)
    for l in range(depth):
        qg = row(jnp.tile(q_norm_g[l], n_rep)) * (LOG2E / math.sqrt(DA_HEAD_DIM))
        kg = row(jnp.tile(k_norm_g[l], n_rep))
        uv, qt, k, vt, gate_pre = _in_projection(x, row(norm1_g[l]), w_in[l].astype(BF16), qg, kg)
        o_da = _attention(slopes, qt, k, vt, row(lam_q1[l]), row(lam_k1[l]), row(lam_q2[l]),
                          row(lam_k2[l]), row(subln_g[l]))
        sg_bias = jnp.repeat(sg_b[l].T.astype(F32), SG_GROUP_DIM, axis=1)
        h = _merge(x, uv, o_da, gate_pre, row(b_gate[l]), row(sg_ln_g[l]), row(sg_ln_b[l]),
                   sg_w[l].astype(BF16), sg_bias,
                   w_proj_sg[l].astype(BF16), w_proj_da[l].astype(BF16), w_out[l].astype(BF16))
        x = _ffn(h, row(norm2_g[l]), w_ffn_gate[l].astype(BF16), w_ffn_up[l].astype(BF16),
                 w_ffn_down[l].astype(BF16))
    return x
```

```python
import functools
import math

import jax
import jax.numpy as jnp
from jax import lax
from jax.experimental import pallas as pl
from jax.experimental.pallas import tpu as pltpu

D_MODEL = 1024
SG_GROUPS = 8
SG_GROUP_DIM = 64
SG_WIDTH = SG_GROUPS * SG_GROUP_DIM
CHUNK = 128
DA_HEADS = 8
DA_HEAD_DIM = 64
DA_V_DIM = 2 * DA_HEAD_DIM
DA_WIDTH = DA_HEADS * DA_V_DIM
EPS = 1e-6
LAMBDA_INIT = 0.8 - 0.6 * math.exp(-0.3 * 0)
LOG2E = math.log2(math.e)

BF16 = jnp.bfloat16
F32 = jnp.float32

LANES = 128
BF16_SUBLANES = 16
MXU_DIM = 256
VMEM_LIMIT_BYTES = 56 * 1024 * 1024

TM_PROJ = 512
TQ = 512
TK = 512
POS_SPLIT = 64
N_AUG = 15
SAFE_EXP2_RANGE = 60.0
KMAX_MARGIN = 1.01
BLOCKS_PER_REGION = 2
assert TQ == TK and N_AUG <= BF16_SUBLANES
TM_MERGE = 512
TM_FFN = 1024
FF_CHUNK = 256


def _dot(a, b):
    return jnp.dot(a, b, preferred_element_type=F32)


def _const_spec(shape):
    n = len(shape)
    return pl.BlockSpec(shape, lambda *_: (0,) * n, pipeline_mode=pl.Buffered(1))


def _group_ones(n, value=1.0):
    r = lax.broadcasted_iota(jnp.int32, (n, n), 0) // DA_HEAD_DIM
    c = lax.broadcasted_iota(jnp.int32, (n, n), 1) // DA_HEAD_DIM
    return jnp.where(r == c, value, 0.0).astype(BF16)


def _inproj_kernel(x_ref, n1g_ref, w_ref, qg_ref, kg_ref,
                   uv_ref, qt_ref, k_ref, vt_ref, gate_ref, xn_ref):
    x = x_ref[...]
    ms = jnp.mean(x * x, axis=-1, keepdims=True)
    xn_ref[...] = (x * lax.rsqrt(ms + EPS) * n1g_ref[...]).astype(BF16)
    averager = _group_ones(MXU_DIM, 1.0 / DA_HEAD_DIM)

    def unit_rms(p, gain_ref):
        sq = (p * p).astype(BF16)
        mean_sq = jnp.concatenate([_dot(sq[:, i * MXU_DIM:(i + 1) * MXU_DIM], averager)
                                   for i in range(p.shape[1] // MXU_DIM)], axis=1)
        return p * lax.rsqrt(mean_sq + EPS) * gain_ref[...]

    def spatial(p):
        uv_ref[...] = p.astype(BF16)

    def query(p):
        qn = unit_rms(p, qg_ref)
        for h in range(DA_HEADS):
            qt_ref[h] = qn[:, h * DA_V_DIM:(h + 1) * DA_V_DIM].T.astype(BF16)

    def key(p):
        kn = unit_rms(p, kg_ref).astype(BF16)
        for h in range(DA_HEADS):
            k_ref[h] = kn[:, h * DA_V_DIM:(h + 1) * DA_V_DIM]

    def value(p):
        for h in range(DA_HEADS):
            vt_ref[h] = p[:, h * DA_V_DIM:(h + 1) * DA_V_DIM].T.astype(BF16)

    def gate_sg(p):
        gate_ref[:, :D_MODEL] = p.astype(BF16)

    def gate_da(p):
        gate_ref[:, D_MODEL:] = p.astype(BF16)

    col = 0
    for width, epilogue in ((2 * SG_WIDTH, spatial), (DA_WIDTH, query), (DA_WIDTH, key), (DA_WIDTH, value),
                            (D_MODEL, gate_sg), (D_MODEL, gate_da)):
        epilogue(_dot(xn_ref[...], w_ref[:, col:col + width]))
        col += width


def _in_projection(x, n1g, w_in, qg, kg):
    B, S, D = x.shape
    tm = TM_PROJ
    n_t = S // tm
    n_kc = S // TK
    sub = TK // tm
    in_cols = w_in.shape[1]
    tok = lambda width: pl.BlockSpec((None, tm, width), lambda b, i: (b, i, 0))
    out_shape = (
        jax.ShapeDtypeStruct((B, S, 2 * SG_WIDTH), BF16),
        jax.ShapeDtypeStruct((B, DA_HEADS, S // TQ, DA_V_DIM, TQ), BF16),
        jax.ShapeDtypeStruct((B, DA_HEADS, S, DA_V_DIM), BF16),
        jax.ShapeDtypeStruct((B, DA_HEADS, n_kc, DA_V_DIM, TK), BF16),
        jax.ShapeDtypeStruct((B, S, 2 * D_MODEL), BF16),
    )
    out_specs = (
        tok(2 * SG_WIDTH),
        pl.BlockSpec((None, DA_HEADS, None, DA_V_DIM, tm), lambda b, i: (b, 0, i // sub, 0, i % sub)),
        pl.BlockSpec((None, DA_HEADS, tm, DA_V_DIM), lambda b, i: (b, 0, i, 0)),
        pl.BlockSpec((None, DA_HEADS, None, DA_V_DIM, tm), lambda b, i: (b, 0, i // sub, 0, i % sub)),
        tok(2 * D_MODEL),
    )
    in_specs = [
        tok(D),
        _const_spec((1, D)),
        _const_spec((D, in_cols)),
        _const_spec((1, DA_WIDTH)),
        _const_spec((1, DA_WIDTH)),
    ]
    return pl.pallas_call(
        _inproj_kernel,
        grid=(B, n_t),
        in_specs=in_specs,
        out_specs=out_specs,
        out_shape=out_shape,
        scratch_shapes=[pltpu.VMEM((tm, D), BF16)],
        compiler_params=pltpu.CompilerParams(vmem_limit_bytes=VMEM_LIMIT_BYTES),
        name="in_projection",
    )(x, n1g, w_in, qg, kg)


def _split3(x):
    hi = x.astype(BF16).astype(F32)
    r = x - hi
    mid = r.astype(BF16).astype(F32)
    lo = (r - mid).astype(BF16).astype(F32)
    return hi, mid, lo


def _alibi_constants(slopes, seq, tq):
    parts = jnp.stack(_split3(slopes), axis=0)
    idx = jnp.arange(LANES, dtype=jnp.int32)
    part_of = parts[idx % 3].T
    kpos = jnp.arange(seq, dtype=jnp.int32)[None, :, None]
    a = idx[None, None, :]
    kcol = jnp.where(a < 3, -1.0, jnp.where(a < 9, -part_of[:, None, :],
           jnp.where(a < 12, ((kpos // POS_SPLIT) * POS_SPLIT).astype(F32),
           jnp.where(a < N_AUG, (kpos % POS_SPLIT).astype(F32), 0.0))))
    qloc = (jnp.arange(2 * tq, dtype=jnp.int32) % tq)[None, None, :]
    a = jnp.arange(BF16_SUBLANES, dtype=jnp.int32)[None, :, None]
    qrow = jnp.where((a >= 3) & (a < 6), ((qloc // POS_SPLIT) * POS_SPLIT).astype(F32),
           jnp.where((a >= 6) & (a < 9), (qloc % POS_SPLIT).astype(F32),
           jnp.where((a >= 9) & (a < N_AUG), part_of[:, :BF16_SUBLANES, None], 0.0)))
    return kcol.astype(BF16), qrow


def _attn_kernel(slopes_ref, qt_ref, k_ref, kcol_ref, qrow_ref, dist_ref, vt_ref,
                 lq1_ref, lk1_ref, lq2_ref, lk2_ref, sg_ref, *rest, n_weights):
    w_f32 = rest[:n_weights]
    o_ref = rest[n_weights]
    w_bf16 = rest[n_weights + 1:2 * n_weights + 1]
    qmain_ref, qaug_ref, acc_ref, kmax_ref, fast_ref = rest[2 * n_weights + 1:]
    for src, dst in zip(w_f32, w_bf16):
        dst[...] = src[...].astype(BF16)

    h = pl.program_id(1)
    n_q, dv, tq = qt_ref.shape
    n_kc, _, tk = vt_ref.shape
    slope = slopes_ref[h]

    def rows_of(index, size):
        start = index * size
        return pl.ds(start if isinstance(start, int) else pl.multiple_of(start, size), size)

    def k_chunk(j):
        return k_ref[rows_of(j, tk), :]

    @pl.when((pl.program_id(0) == 0) & (h == 0))
    def _():
        qaug_ref[:, :, BF16_SUBLANES:, :] = jnp.zeros(
            (qaug_ref.shape[0], 3, LANES - BF16_SUBLANES, 2 * tq), BF16)

    ones_bd = _group_ones(LANES)
    best = jnp.zeros((1, LANES), F32)
    for c in range(n_kc):
        kf = k_ref[c * tk:(c + 1) * tk, :].astype(F32)
        best = jnp.maximum(best, jnp.max(_dot((kf * kf).astype(BF16), ones_bd), axis=0, keepdims=True))
    kmax_ref[0:1, :] = jnp.sqrt(best) * KMAX_MARGIN
    lam = (jnp.exp(jnp.sum(lq1_ref[...] * lk1_ref[...], axis=-1, keepdims=True))
           - jnp.exp(jnp.sum(lq2_ref[...] * lk2_ref[...], axis=-1, keepdims=True)) + LAMBDA_INIT)

    def prepare(blk, slot):
        qt = qt_ref[blk]
        sub = lax.broadcasted_iota(jnp.int32, (dv, tq), 0)
        zero = jnp.zeros_like(qt)
        qmain_ref[slot, :, 0:tq] = jnp.where(sub < DA_HEAD_DIM, qt, zero)
        qmain_ref[slot, :, tq:2 * tq] = jnp.where(sub < DA_HEAD_DIM, zero, qt)
        qsq = qt.astype(F32) * qt.astype(F32)
        n1 = jnp.sqrt(jnp.sum(qsq[:DA_HEAD_DIM], axis=0, keepdims=True))
        n2 = jnp.sqrt(jnp.sum(qsq[DA_HEAD_DIM:], axis=0, keepdims=True))
        m = jnp.concatenate([n1 * kmax_ref[0:1, 0:1], n2 * kmax_ref[0:1, DA_HEAD_DIM:DA_HEAD_DIM + 1]], axis=1)
        a = lax.broadcasted_iota(jnp.int32, (BF16_SUBLANES, 2 * tq), 0)
        positional = qrow_ref[...] + jnp.where((a >= 3) & (a < 6), jnp.asarray(blk * tq, F32), 0.0)
        m_hi, m_mid, m_lo = _split3(m)
        m_rows = jnp.where(a == 0, m_hi, jnp.where(a == 1, m_mid, m_lo))
        for variant, tail in enumerate((positional, -positional, jnp.zeros_like(positional))):
            qaug_ref[slot, variant, 0:BF16_SUBLANES, :] = jnp.where(a < 3, m_rows, tail).astype(BF16)
        fast_ref[slot] = jnp.where(jnp.max(m) <= SAFE_EXP2_RANGE, 1, 0)

    def finish(blk, slot):
        o = acc_ref[slot, 0:dv, :] / acc_ref[slot, dv:dv + 1, :]
        o = o[:, :tq] - lam * o[:, tq:]
        ms = jnp.mean(o * o, axis=0, keepdims=True)
        o = (o * lax.rsqrt(ms + EPS)).T
        o_ref[rows_of(blk, tq), :] = (o * sg_ref[...] * (1.0 - LAMBDA_INIT)).astype(BF16)

    def scores_fast(blk, slot):
        ones_row = jnp.where(lax.broadcasted_iota(jnp.int32, (BF16_SUBLANES, tk), 0) == 0, 1.0, 0.0).astype(BF16)

        def chunk(j, variant, bias):
            k_aug = jnp.concatenate([k_chunk(j), kcol_ref[rows_of(j, tk), :]], axis=1)
            e = _dot(k_aug, jnp.concatenate([qmain_ref[slot], qaug_ref[slot, variant]], axis=0))
            if bias is not None:
                e = e - bias
            p = jnp.exp2(e).astype(BF16)
            return _dot(jnp.concatenate([vt_ref[j], ones_row], axis=0), p)

        diag_bias = dist_ref[...] * slope
        acc = chunk(blk, 2, jnp.concatenate([diag_bias, diag_bias], axis=1))
        for t in range(n_kc - 1):
            j = t + jnp.where(t >= blk, 1, 0)
            acc = acc + chunk(j, jnp.where(j > blk, 1, 0), None)
        acc_ref[slot] = acc

    def scores_online(blk, slot):
        signed =(lax.broadcasted_iota(jnp.int32, (tk, 2 * tq), 1) % tq
                  - lax.broadcasted_iota(jnp.int32, (tk, 2 * tq), 0)).astype(F32) * slope

        def body(j, carry):
            mx, l, acc = carry
            off = jnp.asarray(blk * tq - j * tk, F32) * slope
            s = _dot(k_chunk(j), qmain_ref[slot]) - jnp.abs(signed + off)
            m_new = jnp.maximum(mx, jnp.max(s, axis=0, keepdims=True))
            alpha = jnp.exp2(mx - m_new)
            p = jnp.exp2(s - m_new)
            l = alpha * l + jnp.sum(p, axis=0, keepdims=True)
            acc = alpha * acc + _dot(vt_ref[j], p.astype(BF16))
            return m_new, l, acc

        m0 = jnp.full((1, 2 * tq), -jnp.inf, F32)
        l0 = jnp.zeros((1, 2 * tq), F32)
        a0 = jnp.zeros((dv, 2 * tq), F32)
        _, l, acc = lax.fori_loop(0, n_kc, body, (m0, l0, a0))
        acc_ref[slot, 0:dv, :] = acc
        acc_ref[slot, dv:dv + BF16_SUBLANES, :] = jnp.broadcast_to(l, (BF16_SUBLANES, 2 * tq))

    group = BLOCKS_PER_REGION

    def region(first_blk, cur, oth):
        blks = [first_blk + g for g in range(group)]
        prev = [jnp.maximum(b - group, g) for g, b in enumerate(blks)]
        nxt = [jnp.minimum(b + group, n_q - group + g) for g, b in enumerate(blks)]
        fast = fast_ref[cur[0]] == 1
        for s in cur[1:]:
            fast = fast & (fast_ref[s] == 1)

        def body(scores):
            for g in range(group):
                finish(prev[g], oth[g])
            for g in range(group):
                scores(blks[g], cur[g])
            for g in range(group):
                prepare(nxt[g], oth[g])

        pl.when(fast)(lambda: body(scores_fast))
        pl.when(jnp.logical_not(fast))(lambda: body(scores_online))

    slots_a = tuple(range(group))
    slots_b = tuple(range(group, 2 * group))
    for g in range(group):
        acc_ref[slots_b[g]] = jnp.ones(acc_ref.shape[1:], F32)
        prepare(g, slots_a[g])

    def two_regions(u, carry):
        region(2 * group * u, slots_a, slots_b)
        region(2 * group * u + group, slots_b, slots_a)
        return carry

    lax.fori_loop(0, n_q // (2 * group), two_regions, 0)
    for g in range(group):
        finish(n_q - group + g, slots_b[g])


def _attention(slopes, qt, k, vt, lq1, lk1, lq2, lk2, subln_g, later_weights):
    B, H, n_q, dv, tq = qt.shape
    n_kc = vt.shape[2]
    S = k.shape[2]
    n_slots = 2 * BLOCKS_PER_REGION
    assert n_q % n_slots == 0
    steps = B * H
    slabs = [w.reshape(steps, w.shape[0] // steps, w.shape[1]) for w in later_weights]
    slab_spec = lambda s: pl.BlockSpec((None,) + s.shape[1:], lambda b, h, *_: (b * H + h, 0, 0))
    kcol, qrow = _alibi_constants(slopes, S, tq)
    dist = jnp.abs(jnp.arange(TK, dtype=jnp.int32)[:, None]
                   - jnp.arange(tq, dtype=jnp.int32)[None, :]).astype(F32)
    small = lambda n: pl.BlockSpec((1, n), lambda b, h, *_: (0, 0))
    grid_spec = pltpu.PrefetchScalarGridSpec(
        num_scalar_prefetch=1,
        grid=(B, H),
        in_specs=[
            pl.BlockSpec((None, None, n_q, dv, tq), lambda b, h, *_: (b, h, 0, 0, 0)),
            pl.BlockSpec((None, None, S, dv), lambda b, h, *_: (b, h, 0, 0)),
            pl.BlockSpec((None, S, LANES), lambda b, h, *_: (h, 0, 0)),
            pl.BlockSpec((None, BF16_SUBLANES, 2 * tq), lambda b, h, *_: (h, 0, 0)),
            pl.BlockSpec((TK, tq), lambda b, h, *_: (0, 0)),
            pl.BlockSpec((None, None, n_kc, dv, TK), lambda b, h, *_: (b, h, 0, 0, 0)),
            small(DA_HEAD_DIM), small(DA_HEAD_DIM), small(DA_HEAD_DIM), small(DA_HEAD_DIM),
            small(DA_V_DIM),
        ] + [slab_spec(s) for s in slabs],
        out_specs=[pl.BlockSpec((None, S, dv), lambda b, h, *_: (b, 0, h))] + [slab_spec(s) for s in slabs],
        scratch_shapes=[
            pltpu.VMEM((n_slots, dv, 2 * tq), BF16),
            pltpu.VMEM((n_slots, 3, LANES, 2 * tq), BF16),
            pltpu.VMEM((n_slots, dv + BF16_SUBLANES, 2 * tq), F32),
            pltpu.VMEM((8, LANES), F32),
            pltpu.SMEM((n_slots,), jnp.int32),
        ],
    )
    o_da, *cast = pl.pallas_call(
        functools.partial(_attn_kernel, n_weights=len(slabs)),
        grid_spec=grid_spec,
        out_shape=[jax.ShapeDtypeStruct((B, S, H * dv), BF16)]
                  + [jax.ShapeDtypeStruct(s.shape, BF16) for s in slabs],
        compiler_params=pltpu.CompilerParams(vmem_limit_bytes=VMEM_LIMIT_BYTES),
        name="diff_attention",
    )(slopes, qt, k, kcol, qrow, dist, vt, lq1, lk1, lq2, lk2, subln_g, *slabs)
    return o_da, [c.reshape(w.shape) for c, w in zip(cast, later_weights)]


def _merge_kernel(x_ref, uv_ref, oda_ref, gate_ref, bg_ref, lng_ref, lnb_ref, sw_ref, sb_ref,
                  psg_ref, pda_ref, wo_ref, h_ref, vn_ref):
    tm = x_ref.shape[0]
    n_chunks = tm // CHUNK

    y_da = _dot(oda_ref[...], pda_ref[...])

    gu = jax.nn.gelu(uv_ref[:, :SG_WIDTH].astype(F32))
    gv = jax.nn.gelu(uv_ref[:, SG_WIDTH:].astype(F32))
    mu = jnp.mean(gv, axis=-1, keepdims=True)
    cen = gv - mu
    var = jnp.mean(cen * cen, axis=-1, keepdims=True)
    vn_ref[...] = (cen * lax.rsqrt(var + EPS) * lng_ref[...] + lnb_ref[...]).astype(BF16)

    lane = lax.broadcasted_iota(jnp.int32, (CHUNK, tm), 1)
    first_group = (lane % LANES) < SG_GROUP_DIM
    mixed = []
    for pr in range(SG_GROUPS // 2):
        vp = jnp.concatenate([vn_ref[c * CHUNK:(c + 1) * CHUNK, pr * LANES:(pr + 1) * LANES]
                              for c in range(n_chunks)], axis=1)
        mixed.append(jnp.where(first_group, _dot(sw_ref[2 * pr], vp), _dot(sw_ref[2 * pr + 1], vp)))
    bias = sb_ref[...]
    sv = jnp.concatenate(
        [jnp.concatenate([m[:, c * LANES:(c + 1) * LANES] for m in mixed], axis=1) + bias
         for c in range(n_chunks)], axis=0)
    o_sg = (gu * sv).astype(BF16)

    y_sg = _dot(o_sg, psg_ref[...])
    g_sg = jax.nn.sigmoid(gate_ref[:, :D_MODEL].astype(F32) + bg_ref[:, :D_MODEL])
    g_da = jax.nn.sigmoid(gate_ref[:, D_MODEL:].astype(F32) + bg_ref[:, D_MODEL:])
    z = (g_sg * y_sg + g_da * y_da).astype(BF16)
    h_ref[...] = x_ref[...] + _dot(z, wo_ref[...])


def _merge(x, uv, o_da, gate_pre, b_gate, ln_g, ln_b, sg_w, sg_bias, p_sg, p_da, w_out):
    B, S, D = x.shape
    tm = TM_MERGE
    tok = lambda width: pl.BlockSpec((None, tm, width), lambda b, i: (b, i, 0))
    return pl.pallas_call(
        _merge_kernel,
        grid=(B, S // tm),
        in_specs=[
            tok(D), tok(2 * SG_WIDTH), tok(DA_WIDTH), tok(2 * D_MODEL),
            _const_spec(b_gate.shape), _const_spec(ln_g.shape), _const_spec(ln_b.shape),
            _const_spec(sg_w.shape), _const_spec(sg_bias.shape),
            _const_spec(p_sg.shape), _const_spec(p_da.shape), _const_spec(w_out.shape),
        ],
        out_specs=tok(D),
        out_shape=jax.ShapeDtypeStruct((B, S, D), F32),
        scratch_shapes=[pltpu.VMEM((tm, SG_WIDTH), BF16)],
        compiler_params=pltpu.CompilerParams(vmem_limit_bytes=VMEM_LIMIT_BYTES),
        name="merge",
    )(x, uv, o_da, gate_pre, b_gate, ln_g, ln_b, sg_w, sg_bias, p_sg, p_da, w_out)


def _ffn_kernel(h_ref, g_ref, wg_ref, wu_ref, wd_ref, o_ref):
    hres = h_ref[...]
    ms = jnp.mean(hres * hres, axis=-1, keepdims=True)
    hn = (hres * lax.rsqrt(ms + EPS) * g_ref[...]).astype(BF16)
    d_ff = wg_ref.shape[1]
    acc = hres
    for c in range(d_ff // FF_CHUNK):
        sl = slice(c * FF_CHUNK, (c + 1) * FF_CHUNK)
        a = jax.nn.silu(_dot(hn, wg_ref[:, sl])) * _dot(hn, wu_ref[:, sl])
        acc = acc + _dot(a.astype(BF16), wd_ref[sl, :])
    o_ref[...] = acc


def _ffn(h, g, w_gate, w_up, w_down):
    B, S, D = h.shape
    tm = TM_FFN
    tok = pl.BlockSpec((None, tm, D), lambda b, i: (b, i, 0))
    return pl.pallas_call(
        _ffn_kernel,
        grid=(B, S // tm),
        in_specs=[tok, _const_spec((1, D)), _const_spec(w_gate.shape), _const_spec(w_up.shape),
                  _const_spec(w_down.shape)],
        out_specs=tok,
        out_shape=jax.ShapeDtypeStruct((B, S, D), F32),
        compiler_params=pltpu.CompilerParams(vmem_limit_bytes=VMEM_LIMIT_BYTES),
        name="swiglu_ffn",
    )(h, g, w_gate, w_up, w_down)


def kernel(x, norm1_g, w_in, b_gate, sg_ln_g, sg_ln_b, sg_w, sg_b, q_norm_g, k_norm_g, lam_q1, lam_k1, lam_q2, lam_k2, subln_g, w_proj_sg, w_proj_da, w_out, norm2_g, w_ffn_gate, w_ffn_up, w_ffn_down):
    depth = w_in.shape[0]
    row = lambda a: a.reshape(1, -1).astype(F32)
    n_rep = DA_WIDTH // DA_HEAD_DIM
    slopes = jnp.asarray([2.0 ** (-8.0 * (i + 1) / DA_HEADS) * LOG2E for i in range(DA_HEADS)], F32)
    for l in range(depth):
        qg = row(jnp.tile(q_norm_g[l], n_rep)) * (LOG2E / math.sqrt(DA_HEAD_DIM))
        kg = row(jnp.tile(k_norm_g[l], n_rep))
        uv, qt, k, vt, gate_pre = _in_projection(x, row(norm1_g[l]), w_in[l].astype(BF16), qg, kg)
        o_da, (p_sg, p_da, w_o, w_gate, w_up, w_down) = _attention(
            slopes, qt, k, vt, row(lam_q1[l]), row(lam_k1[l]), row(lam_q2[l]), row(lam_k2[l]),
            row(subln_g[l]),
            [w_proj_sg[l], w_proj_da[l], w_out[l], w_ffn_gate[l], w_ffn_up[l], w_ffn_down[l]])
        sg_bias = jnp.repeat(sg_b[l].T.astype(F32), SG_GROUP_DIM, axis=1)
        h = _merge(x, uv, o_da, gate_pre, row(b_gate[l]), row(sg_ln_g[l]), row(sg_ln_b[l]),
                   sg_w[l].astype(BF16), sg_bias, p_sg, p_da, w_o)
        x = _ffn(h, row(norm2_g[l]), w_gate, w_up, w_down)
    return x
```

```python
import functools
import math

import numpy as np
import jax
import jax.numpy as jnp
from jax import lax
from jax.experimental import pallas as pl
from jax.experimental.pallas import tpu as pltpu

D_MODEL = 1024
SG_GROUPS = 8
SG_GROUP_DIM = 64
SG_WIDTH = SG_GROUPS * SG_GROUP_DIM
CHUNK = 128
DA_HEADS = 8
DA_HEAD_DIM = 64
DA_V_DIM = 2 * DA_HEAD_DIM
DA_WIDTH = DA_HEADS * DA_V_DIM
EPS = 1e-6
LAMBDA_INIT = 0.8 - 0.6 * math.exp(-0.3 * 0)
LOG2E = math.log2(math.e)

BF16 = jnp.bfloat16
F32 = jnp.float32

LANES = 128
BF16_SUBLANES = 16
MXU_DIM = 256
VMEM_LIMIT_BYTES = 56 * 1024 * 1024

TM_PROJ = 512
TQ = 512
TK = 512
POS_SPLIT = 64
N_AUG = 15
SAFE_EXP2_RANGE = 60.0
KMAX_MARGIN = 1.01
BLOCKS_PER_REGION = 2
assert TQ == TK and N_AUG <= BF16_SUBLANES
TM_MERGE = 512
TM_FFN = 1024
FF_CHUNK = 256


def _dot(a, b):
    return jnp.dot(a, b, preferred_element_type=F32)


def _const_spec(shape):
    n = len(shape)
    return pl.BlockSpec(shape, lambda *_: (0,) * n, pipeline_mode=pl.Buffered(1))


def _group_ones(n, value=1.0):
    r = lax.broadcasted_iota(jnp.int32, (n, n), 0) // DA_HEAD_DIM
    c = lax.broadcasted_iota(jnp.int32, (n, n), 1) // DA_HEAD_DIM
    return jnp.where(r == c, value, 0.0).astype(BF16)


def _inproj_kernel(x_ref, n1g_ref, w_ref, qg_ref, kg_ref,
                   uv_ref, qt_ref, k_ref, vt_ref, gate_ref, xn_ref):
    x = x_ref[...]
    ms = jnp.mean(x * x, axis=-1, keepdims=True)
    xn_ref[...] = (x * lax.rsqrt(ms + EPS) * n1g_ref[...]).astype(BF16)
    averager = _group_ones(MXU_DIM, 1.0 / DA_HEAD_DIM)

    def unit_rms(p, gain_ref):
        sq = (p * p).astype(BF16)
        mean_sq = jnp.concatenate([_dot(sq[:, i * MXU_DIM:(i + 1) * MXU_DIM], averager)
                                   for i in range(p.shape[1] // MXU_DIM)], axis=1)
        return p * lax.rsqrt(mean_sq + EPS) * gain_ref[...]

    def spatial(p):
        uv_ref[...] = p.astype(BF16)

    def query(p):
        qn = unit_rms(p, qg_ref)
        for h in range(DA_HEADS):
            qt_ref[h] = qn[:, h * DA_V_DIM:(h + 1) * DA_V_DIM].T.astype(BF16)

    def key(p):
        kn = unit_rms(p, kg_ref).astype(BF16)
        for h in range(DA_HEADS):
            k_ref[h] = kn[:, h * DA_V_DIM:(h + 1) * DA_V_DIM]

    def value(p):
        for h in range(DA_HEADS):
            vt_ref[h] = p[:, h * DA_V_DIM:(h + 1) * DA_V_DIM].T.astype(BF16)

    def gate_sg(p):
        gate_ref[:, :D_MODEL] = p.astype(BF16)

    def gate_da(p):
        gate_ref[:, D_MODEL:] = p.astype(BF16)

    col = 0
    for width, epilogue in ((2 * SG_WIDTH, spatial), (DA_WIDTH, query), (DA_WIDTH, key), (DA_WIDTH, value),
                            (D_MODEL, gate_sg), (D_MODEL, gate_da)):
        epilogue(_dot(xn_ref[...], w_ref[:, col:col + width]))
        col += width


def _in_projection(x, n1g, w_in, qg, kg):
    B, S, D = x.shape
    tm = TM_PROJ
    n_t = S // tm
    n_kc = S // TK
    sub = TK // tm
    in_cols = w_in.shape[1]
    tok = lambda width: pl.BlockSpec((None, tm, width), lambda b, i: (b, i, 0))
    out_shape = (
        jax.ShapeDtypeStruct((B, S, 2 * SG_WIDTH), BF16),
        jax.ShapeDtypeStruct((B, DA_HEADS, S // TQ, DA_V_DIM, TQ), BF16),
        jax.ShapeDtypeStruct((B, DA_HEADS, S, DA_V_DIM), BF16),
        jax.ShapeDtypeStruct((B, DA_HEADS, n_kc, DA_V_DIM, TK), BF16),
        jax.ShapeDtypeStruct((B, S, 2 * D_MODEL), BF16),
    )
    out_specs = (
        tok(2 * SG_WIDTH),
        pl.BlockSpec((None, DA_HEADS, None, DA_V_DIM, tm), lambda b, i: (b, 0, i // sub, 0, i % sub)),
        pl.BlockSpec((None, DA_HEADS, tm, DA_V_DIM), lambda b, i: (b, 0, i, 0)),
        pl.BlockSpec((None, DA_HEADS, None, DA_V_DIM, tm), lambda b, i: (b, 0, i // sub, 0, i % sub)),
        tok(2 * D_MODEL),
    )
    in_specs = [
        tok(D),
        _const_spec((1, D)),
        _const_spec((D, in_cols)),
        _const_spec((1, DA_WIDTH)),
        _const_spec((1, DA_WIDTH)),
    ]
    return pl.pallas_call(
        _inproj_kernel,
        grid=(B, n_t),
        in_specs=in_specs,
        out_specs=out_specs,
        out_shape=out_shape,
        scratch_shapes=[pltpu.VMEM((tm, D), BF16)],
        compiler_params=pltpu.CompilerParams(vmem_limit_bytes=VMEM_LIMIT_BYTES),
        name="in_projection",
    )(x, n1g, w_in, qg, kg)


def _split3(x):
    hi = x.astype(BF16).astype(F32)
    r = x - hi
    mid = r.astype(BF16).astype(F32)
    lo = (r - mid).astype(BF16).astype(F32)
    return hi, mid, lo


def _alibi_constants(slopes, seq, tq):
    parts = np.stack(_split3(np.asarray(slopes, np.float32)), axis=0)
    idx = np.arange(LANES, dtype=np.int32)
    part_of = parts[idx % 3].T
    kpos = np.arange(seq, dtype=np.int32)[None, :, None]
    a = idx[None, None, :]
    kcol = np.where(a < 3, -1.0, np.where(a < 9, -part_of[:, None, :],
           np.where(a < 12, ((kpos // POS_SPLIT) * POS_SPLIT).astype(np.float32),
           np.where(a < N_AUG, (kpos % POS_SPLIT).astype(np.float32), 0.0))))
    qloc = (np.arange(2 * tq, dtype=np.int32) % tq)[None, None, :]
    a = np.arange(BF16_SUBLANES, dtype=np.int32)[None, :, None]
    qrow = np.where((a >= 3) & (a < 6), ((qloc // POS_SPLIT) * POS_SPLIT).astype(np.float32),
           np.where((a >= 6) & (a < 9), (qloc % POS_SPLIT).astype(np.float32),
           np.where((a >= 9) & (a < N_AUG), part_of[:, :BF16_SUBLANES, None], 0.0)))
    return kcol.astype(BF16), qrow.astype(np.float32)


def _attn_kernel(slopes_ref, qt_ref, k_ref, kcol_ref, qrow_ref, dist_ref, vt_ref,
                 lq1_ref, lk1_ref, lq2_ref, lk2_ref, sg_ref, *rest, n_weights):
    w_f32 = rest[:n_weights]
    o_ref = rest[n_weights]
    w_bf16 = rest[n_weights + 1:2 * n_weights + 1]
    qmain_ref, qaug_ref, acc_ref, kmax_ref, fast_ref = rest[2 * n_weights + 1:]
    for src, dst in zip(w_f32, w_bf16):
        dst[...] = src[...].astype(BF16)

    h = pl.program_id(1)
    n_q, dv, tq = qt_ref.shape
    n_kc, _, tk = vt_ref.shape
    slope = slopes_ref[h]

    def rows_of(index, size):
        start = index * size
        return pl.ds(start if isinstance(start, int) else pl.multiple_of(start, size), size)

    def k_chunk(j):
        return k_ref[rows_of(j, tk), :]

    @pl.when((pl.program_id(0) == 0) & (h == 0))
    def _():
        qaug_ref[:, :, BF16_SUBLANES:, :] = jnp.zeros(
            (qaug_ref.shape[0], 3, LANES - BF16_SUBLANES, 2 * tq), BF16)

    ones_bd = _group_ones(LANES)
    best = jnp.zeros((1, LANES), F32)
    for c in range(n_kc):
        kf = k_ref[c * tk:(c + 1) * tk, :].astype(F32)
        best = jnp.maximum(best, jnp.max(_dot((kf * kf).astype(BF16), ones_bd), axis=0, keepdims=True))
    kmax_ref[0:1, :] = jnp.sqrt(best) * KMAX_MARGIN
    lam = (jnp.exp(jnp.sum(lq1_ref[...] * lk1_ref[...], axis=-1, keepdims=True))
           - jnp.exp(jnp.sum(lq2_ref[...] * lk2_ref[...], axis=-1, keepdims=True)) + LAMBDA_INIT)

    def prepare(blk, slot):
        qt = qt_ref[blk]
        sub = lax.broadcasted_iota(jnp.int32, (dv, tq), 0)
        zero = jnp.zeros_like(qt)
        qmain_ref[slot, :, 0:tq] = jnp.where(sub < DA_HEAD_DIM, qt, zero)
        qmain_ref[slot, :, tq:2 * tq] = jnp.where(sub < DA_HEAD_DIM, zero, qt)
        qsq = qt.astype(F32) * qt.astype(F32)
        n1 = jnp.sqrt(jnp.sum(qsq[:DA_HEAD_DIM], axis=0, keepdims=True))
        n2 = jnp.sqrt(jnp.sum(qsq[DA_HEAD_DIM:], axis=0, keepdims=True))
        m = jnp.concatenate([n1 * kmax_ref[0:1, 0:1], n2 * kmax_ref[0:1, DA_HEAD_DIM:DA_HEAD_DIM + 1]], axis=1)
        a = lax.broadcasted_iota(jnp.int32, (BF16_SUBLANES, 2 * tq), 0)
        positional = qrow_ref[...] + jnp.where((a >= 3) & (a < 6), jnp.asarray(blk * tq, F32), 0.0)
        m_hi, m_mid, m_lo = _split3(m)
        m_rows = jnp.where(a == 0, m_hi, jnp.where(a == 1, m_mid, m_lo))
        for variant, tail in enumerate((positional, -positional, jnp.zeros_like(positional))):
            qaug_ref[slot, variant, 0:BF16_SUBLANES, :] = jnp.where(a < 3, m_rows, tail).astype(BF16)
        fast_ref[slot] = jnp.where(jnp.max(m) <= SAFE_EXP2_RANGE, 1, 0)

    def finish(blk, slot):
        o = acc_ref[slot, 0:dv, :] / acc_ref[slot, dv:dv + 1, :]
        o = o[:, :tq] - lam * o[:, tq:]
        ms = jnp.mean(o * o, axis=0, keepdims=True)
        o = (o * lax.rsqrt(ms + EPS)).T
        o_ref[rows_of(blk, tq), :] = (o * sg_ref[...] * (1.0 - LAMBDA_INIT)).astype(BF16)

    def scores_fast(blk, slot):
        ones_row = jnp.where(lax.broadcasted_iota(jnp.int32, (BF16_SUBLANES, tk), 0) == 0, 1.0, 0.0).astype(BF16)

        def chunk(j, variant, bias):
            k_aug = jnp.concatenate([k_chunk(j), kcol_ref[rows_of(j, tk), :]], axis=1)
            e = _dot(k_aug, jnp.concatenate([qmain_ref[slot], qaug_ref[slot, variant]], axis=0))
            if bias is not None:
                e = e - bias
            p = jnp.exp2(e).astype(BF16)
            return _dot(jnp.concatenate([vt_ref[j], ones_row], axis=0), p)

        diag_bias = dist_ref[...] * slope
        acc = chunk(blk, 2, jnp.concatenate([diag_bias, diag_bias], axis=1))
        for t in range(n_kc - 1):
            j = t + jnp.where(t >= blk, 1, 0)
            acc = acc + chunk(j, jnp.where(j > blk, 1, 0), None)
        acc_ref[slot] = acc

    def scores_online(blk, slot):
        signed =(lax.broadcasted_iota(jnp.int32, (tk, 2 * tq), 1) % tq
                  - lax.broadcasted_iota(jnp.int32, (tk, 2 * tq), 0)).astype(F32) * slope

        def body(j, carry):
            mx, l, acc = carry
            off = jnp.asarray(blk * tq - j * tk, F32) * slope
            s = _dot(k_chunk(j), qmain_ref[slot]) - jnp.abs(signed + off)
            m_new = jnp.maximum(mx, jnp.max(s, axis=0, keepdims=True))
            alpha = jnp.exp2(mx - m_new)
            p = jnp.exp2(s - m_new)
            l = alpha * l + jnp.sum(p, axis=0, keepdims=True)
            acc = alpha * acc + _dot(vt_ref[j], p.astype(BF16))
            return m_new, l, acc

        m0 = jnp.full((1, 2 * tq), -jnp.inf, F32)
        l0 = jnp.zeros((1, 2 * tq), F32)
        a0 = jnp.zeros((dv, 2 * tq), F32)
        _, l, acc = lax.fori_loop(0, n_kc, body, (m0, l0, a0))
        acc_ref[slot, 0:dv, :] = acc
        acc_ref[slot, dv:dv + BF16_SUBLANES, :] = jnp.broadcast_to(l, (BF16_SUBLANES, 2 * tq))

    group = BLOCKS_PER_REGION

    def region(first_blk, cur, oth):
        blks = [first_blk + g for g in range(group)]
        prev = [jnp.maximum(b - group, g) for g, b in enumerate(blks)]
        nxt = [jnp.minimum(b + group, n_q - group + g) for g, b in enumerate(blks)]
        fast = fast_ref[cur[0]] == 1
        for s in cur[1:]:
            fast = fast & (fast_ref[s] == 1)

        def body(scores):
            for g in range(group):
                finish(prev[g], oth[g])
            for g in range(group):
                scores(blks[g], cur[g])
            for g in range(group):
                prepare(nxt[g], oth[g])

        pl.when(fast)(lambda: body(scores_fast))
        pl.when(jnp.logical_not(fast))(lambda: body(scores_online))

    slots_a = tuple(range(group))
    slots_b = tuple(range(group, 2 * group))
    for g in range(group):
        acc_ref[slots_b[g]] = jnp.ones(acc_ref.shape[1:], F32)
        prepare(g, slots_a[g])

    def two_regions(u, carry):
        region(2 * group * u, slots_a, slots_b)
        region(2 * group * u + group, slots_b, slots_a)
        return carry

    lax.fori_loop(0, n_q // (2 * group), two_regions, 0)
    for g in range(group):
        finish(n_q - group + g, slots_b[g])


def _attention(slopes, qt, k, vt, lq1, lk1, lq2, lk2, subln_g, later_weights):
    B, H, n_q, dv, tq = qt.shape
    n_kc = vt.shape[2]
    S = k.shape[2]
    n_slots = 2 * BLOCKS_PER_REGION
    assert n_q % n_slots == 0
    steps = B * H
    slabs = [w.reshape(steps, w.shape[0] // steps, w.shape[1]) for w in later_weights]
    slab_spec = lambda s: pl.BlockSpec((None,) + s.shape[1:], lambda b, h, *_: (b * H + h, 0, 0))
    kcol, qrow = _alibi_constants(slopes, S, tq)
    dist = np.abs(np.arange(TK, dtype=np.int32)[:, None]
                  - np.arange(tq, dtype=np.int32)[None, :]).astype(np.float32)
    slopes = np.asarray(slopes, np.float32)
    small = lambda n: pl.BlockSpec((1, n), lambda b, h, *_: (0, 0))
    grid_spec = pltpu.PrefetchScalarGridSpec(
        num_scalar_prefetch=1,
        grid=(B, H),
        in_specs=[
            pl.BlockSpec((None, None, n_q, dv, tq), lambda b, h, *_: (b, h, 0, 0, 0)),
            pl.BlockSpec((None, None, S, dv), lambda b, h, *_: (b, h, 0, 0)),
            pl.BlockSpec((None, S, LANES), lambda b, h, *_: (h, 0, 0)),
            pl.BlockSpec((None, BF16_SUBLANES, 2 * tq), lambda b, h, *_: (h, 0, 0)),
            pl.BlockSpec((TK, tq), lambda b, h, *_: (0, 0)),
            pl.BlockSpec((None, None, n_kc, dv, TK), lambda b, h, *_: (b, h, 0, 0, 0)),
            small(DA_HEAD_DIM), small(DA_HEAD_DIM), small(DA_HEAD_DIM), small(DA_HEAD_DIM),
            small(DA_V_DIM),
        ] + [slab_spec(s) for s in slabs],
        out_specs=[pl.BlockSpec((None, S, dv), lambda b, h, *_: (b, 0, h))] + [slab_spec(s) for s in slabs],
        scratch_shapes=[
            pltpu.VMEM((n_slots, dv, 2 * tq), BF16),
            pltpu.VMEM((n_slots, 3, LANES, 2 * tq), BF16),
            pltpu.VMEM((n_slots, dv + BF16_SUBLANES, 2 * tq), F32),
            pltpu.VMEM((8, LANES), F32),
            pltpu.SMEM((n_slots,), jnp.int32),
        ],
    )
    o_da, *cast = pl.pallas_call(
        functools.partial(_attn_kernel, n_weights=len(slabs)),
        grid_spec=grid_spec,
        out_shape=[jax.ShapeDtypeStruct((B, S, H * dv), BF16)]
                  + [jax.ShapeDtypeStruct(s.shape, BF16) for s in slabs],
        compiler_params=pltpu.CompilerParams(vmem_limit_bytes=VMEM_LIMIT_BYTES),
        name="diff_attention",
    )(slopes, qt, k, kcol, qrow, dist, vt, lq1, lk1, lq2, lk2, subln_g, *slabs)
    return o_da, [c.reshape(w.shape) for c, w in zip(cast, later_weights)]


def _merge_kernel(x_ref, uv_ref, oda_ref, gate_ref, bg_ref, lng_ref, lnb_ref, sw_ref, sb_ref,
                  psg_ref, pda_ref, wo_ref, h_ref, vn_ref):
    tm = x_ref.shape[0]
    n_chunks = tm // CHUNK

    y_da = _dot(oda_ref[...], pda_ref[...])

    gu = jax.nn.gelu(uv_ref[:, :SG_WIDTH].astype(F32))
    gv = jax.nn.gelu(uv_ref[:, SG_WIDTH:].astype(F32))
    mu = jnp.mean(gv, axis=-1, keepdims=True)
    cen = gv - mu
    var = jnp.mean(cen * cen, axis=-1, keepdims=True)
    vn_ref[...] = (cen * lax.rsqrt(var + EPS) * lng_ref[...] + lnb_ref[...]).astype(BF16)

    lane = lax.broadcasted_iota(jnp.int32, (CHUNK, tm), 1)
    first_group = (lane % LANES) < SG_GROUP_DIM
    mixed = []
    for pr in range(SG_GROUPS // 2):
        vp = jnp.concatenate([vn_ref[c * CHUNK:(c + 1) * CHUNK, pr * LANES:(pr + 1) * LANES]
                              for c in range(n_chunks)], axis=1)
        mixed.append(jnp.where(first_group, _dot(sw_ref[2 * pr], vp), _dot(sw_ref[2 * pr + 1], vp)))
    bias = sb_ref[...]
    sv = jnp.concatenate(
        [jnp.concatenate([m[:, c * LANES:(c + 1) * LANES] for m in mixed], axis=1) + bias
         for c in range(n_chunks)], axis=0)
    o_sg = (gu * sv).astype(BF16)

    y_sg = _dot(o_sg, psg_ref[...])
    g_sg = jax.nn.sigmoid(gate_ref[:, :D_MODEL].astype(F32) + bg_ref[:, :D_MODEL])
    g_da = jax.nn.sigmoid(gate_ref[:, D_MODEL:].astype(F32) + bg_ref[:, D_MODEL:])
    z = (g_sg * y_sg + g_da * y_da).astype(BF16)
    h_ref[...] = x_ref[...] + _dot(z, wo_ref[...])


def _merge(x, uv, o_da, gate_pre, b_gate, ln_g, ln_b, sg_w, sg_bias, p_sg, p_da, w_out):
    B, S, D = x.shape
    tm = TM_MERGE
    tok = lambda width: pl.BlockSpec((None, tm, width), lambda b, i: (b, i, 0))
    return pl.pallas_call(
        _merge_kernel,
        grid=(B, S // tm),
        in_specs=[
            tok(D), tok(2 * SG_WIDTH), tok(DA_WIDTH), tok(2 * D_MODEL),
            _const_spec(b_gate.shape), _const_spec(ln_g.shape), _const_spec(ln_b.shape),
            _const_spec(sg_w.shape), _const_spec(sg_bias.shape),
            _const_spec(p_sg.shape), _const_spec(p_da.shape), _const_spec(w_out.shape),
        ],
        out_specs=tok(D),
        out_shape=jax.ShapeDtypeStruct((B, S, D), F32),
        scratch_shapes=[pltpu.VMEM((tm, SG_WIDTH), BF16)],
        compiler_params=pltpu.CompilerParams(vmem_limit_bytes=VMEM_LIMIT_BYTES),
        name="merge",
    )(x, uv, o_da, gate_pre, b_gate, ln_g, ln_b, sg_w, sg_bias, p_sg, p_da, w_out)


def _ffn_kernel(h_ref, g_ref, wg_ref, wu_ref, wd_ref, o_ref):
    hres = h_ref[...]
    ms = jnp.mean(hres * hres, axis=-1, keepdims=True)
    hn = (hres * lax.rsqrt(ms + EPS) * g_ref[...]).astype(BF16)
    d_ff = wg_ref.shape[1]
    acc = hres
    for c in range(d_ff // FF_CHUNK):
        sl = slice(c * FF_CHUNK, (c + 1) * FF_CHUNK)
        a = jax.nn.silu(_dot(hn, wg_ref[:, sl])) * _dot(hn, wu_ref[:, sl])
        acc = acc + _dot(a.astype(BF16), wd_ref[sl, :])
    o_ref[...] = acc


def _ffn(h, g, w_gate, w_up, w_down):
    B, S, D = h.shape
    tm = TM_FFN
    tok = pl.BlockSpec((None, tm, D), lambda b, i: (b, i, 0))
    return pl.pallas_call(
        _ffn_kernel,
        grid=(B, S // tm),
        in_specs=[tok, _const_spec((1, D)), _const_spec(w_gate.shape), _const_spec(w_up.shape),
                  _const_spec(w_down.shape)],
        out_specs=tok,
        out_shape=jax.ShapeDtypeStruct((B, S, D), F32),
        compiler_params=pltpu.CompilerParams(vmem_limit_bytes=VMEM_LIMIT_BYTES),
        name="swiglu_ffn",
    )(h, g, w_gate, w_up, w_down)


def kernel(x, norm1_g, w_in, b_gate, sg_ln_g, sg_ln_b, sg_w, sg_b, q_norm_g, k_norm_g, lam_q1, lam_k1, lam_q2, lam_k2, subln_g, w_proj_sg, w_proj_da, w_out, norm2_g, w_ffn_gate, w_ffn_up, w_ffn_down):
    depth = w_in.shape[0]
    row = lambda a: a.reshape(1, -1).astype(F32)
    n_rep = DA_WIDTH // DA_HEAD_DIM
    slopes = [2.0 ** (-8.0 * (i + 1) / DA_HEADS) * LOG2E for i in range(DA_HEADS)]
    for l in range(depth):
        qg = row(jnp.tile(q_norm_g[l], n_rep)) * (LOG2E / math.sqrt(DA_HEAD_DIM))
        kg = row(jnp.tile(k_norm_g[l], n_rep))
        uv, qt, k, vt, gate_pre = _in_projection(x, row(norm1_g[l]), w_in[l].astype(BF16), qg, kg)
        o_da, (p_sg, p_da, w_o, w_gate, w_up, w_down) = _attention(
            slopes, qt, k, vt, row(lam_q1[l]), row(lam_k1[l]), row(lam_q2[l]), row(lam_k2[l]),
            row(subln_g[l]),
            [w_proj_sg[l], w_proj_da[l], w_out[l], w_ffn_gate[l], w_ffn_up[l], w_ffn_down[l]])
        sg_bias = jnp.repeat(sg_b[l].T.astype(F32), SG_GROUP_DIM, axis=1)
        h = _merge(x, uv, o_da, gate_pre, row(b_gate[l]), row(sg_ln_g[l]), row(sg_ln_b[l]),
                   sg_w[l].astype(BF16), sg_bias, p_sg, p_da, w_o)
        x = _ffn(h, row(norm2_g[l]), w_gate, w_up, w_down)
    return x
```

```python
import functools
import math

import numpy as np
import jax
import jax.numpy as jnp
from jax import lax
from jax.experimental import pallas as pl
from jax.experimental.pallas import tpu as pltpu

D_MODEL = 1024
SG_GROUPS = 8
SG_GROUP_DIM = 64
SG_WIDTH = SG_GROUPS * SG_GROUP_DIM
CHUNK = 128
DA_HEADS = 8
DA_HEAD_DIM = 64
DA_V_DIM = 2 * DA_HEAD_DIM
DA_WIDTH = DA_HEADS * DA_V_DIM
EPS = 1e-6
LAMBDA_INIT = 0.8 - 0.6 * math.exp(-0.3 * 0)
LOG2E = math.log2(math.e)

BF16 = jnp.bfloat16
F32 = jnp.float32

LANES = 128
BF16_SUBLANES = 16
MXU_DIM = 256
VMEM_LIMIT_BYTES = 56 * 1024 * 1024

TM_PROJ = 512
TQ = 512
TK = 512
POS_SPLIT = 64
N_AUG = 15
SAFE_EXP2_RANGE = 60.0
KMAX_MARGIN = 1.01
BLOCKS_PER_REGION = 2
assert TQ == TK and N_AUG <= BF16_SUBLANES
TM_MERGE = 512
TM_FFN = 1024
FF_CHUNK = 256


def _dot(a, b):
    return jnp.dot(a, b, preferred_element_type=F32)


def _const_spec(shape):
    n = len(shape)
    return pl.BlockSpec(shape, lambda *_: (0,) * n, pipeline_mode=pl.Buffered(1))


def _group_ones(n, value=1.0):
    r = lax.broadcasted_iota(jnp.int32, (n, n), 0) // DA_HEAD_DIM
    c = lax.broadcasted_iota(jnp.int32, (n, n), 1) // DA_HEAD_DIM
    return jnp.where(r == c, value, 0.0).astype(BF16)


def _inproj_kernel(x_ref, n1g_ref, w_ref, qg_ref, kg_ref,
                   uv_ref, qt_ref, k_ref, vt_ref, gate_ref, xn_ref):
    x = x_ref[...]
    ms = jnp.mean(x * x, axis=-1, keepdims=True)
    xn_ref[...] = (x * lax.rsqrt(ms + EPS) * n1g_ref[...]).astype(BF16)
    averager = _group_ones(MXU_DIM, 1.0 / DA_HEAD_DIM)

    def unit_rms(p, gain_ref):
        sq = (p * p).astype(BF16)
        mean_sq = jnp.concatenate([_dot(sq[:, i * MXU_DIM:(i + 1) * MXU_DIM], averager)
                                   for i in range(p.shape[1] // MXU_DIM)], axis=1)
        return p * lax.rsqrt(mean_sq + EPS) * gain_ref[...]

    def spatial(p):
        uv_ref[...] = p.astype(BF16)

    def query(p):
        qn = unit_rms(p, qg_ref)
        for h in range(DA_HEADS):
            qt_ref[h] = qn[:, h * DA_V_DIM:(h + 1) * DA_V_DIM].T.astype(BF16)

    def key(p):
        kn = unit_rms(p, kg_ref).astype(BF16)
        for h in range(DA_HEADS):
            k_ref[h] = kn[:, h * DA_V_DIM:(h + 1) * DA_V_DIM]

    def value(p):
        for h in range(DA_HEADS):
            vt_ref[h] = p[:, h * DA_V_DIM:(h + 1) * DA_V_DIM].T.astype(BF16)

    def gate_sg(p):
        gate_ref[:, :D_MODEL] = p.astype(BF16)

    def gate_da(p):
        gate_ref[:, D_MODEL:] = p.astype(BF16)

    col = 0
    for width, epilogue in ((2 * SG_WIDTH, spatial), (DA_WIDTH, query), (DA_WIDTH, key), (DA_WIDTH, value),
                            (D_MODEL, gate_sg), (D_MODEL, gate_da)):
        epilogue(_dot(xn_ref[...], w_ref[:, col:col + width]))
        col += width


def _in_projection(x, n1g, w_in, qg, kg):
    B, S, D = x.shape
    tm = TM_PROJ
    n_t = S // tm
    n_kc = S // TK
    sub = TK // tm
    in_cols = w_in.shape[1]
    tok = lambda width: pl.BlockSpec((None, tm, width), lambda b, i: (b, i, 0))
    out_shape = (
        jax.ShapeDtypeStruct((B, S, 2 * SG_WIDTH), BF16),
        jax.ShapeDtypeStruct((B, DA_HEADS, S // TQ, DA_V_DIM, TQ), BF16),
        jax.ShapeDtypeStruct((B, DA_HEADS, S, DA_V_DIM), BF16),
        jax.ShapeDtypeStruct((B, DA_HEADS, n_kc, DA_V_DIM, TK), BF16),
        jax.ShapeDtypeStruct((B, S, 2 * D_MODEL), BF16),
    )
    out_specs = (
        tok(2 * SG_WIDTH),
        pl.BlockSpec((None, DA_HEADS, None, DA_V_DIM, tm), lambda b, i: (b, 0, i // sub, 0, i % sub)),
        pl.BlockSpec((None, DA_HEADS, tm, DA_V_DIM), lambda b, i: (b, 0, i, 0)),
        pl.BlockSpec((None, DA_HEADS, None, DA_V_DIM, tm), lambda b, i: (b, 0, i // sub, 0, i % sub)),
        tok(2 * D_MODEL),
    )
    in_specs = [
        tok(D),
        _const_spec((1, D)),
        _const_spec((D, in_cols)),
        _const_spec((1, DA_WIDTH)),
        _const_spec((1, DA_WIDTH)),
    ]
    return pl.pallas_call(
        _inproj_kernel,
        grid=(B, n_t),
        in_specs=in_specs,
        out_specs=out_specs,
        out_shape=out_shape,
        scratch_shapes=[pltpu.VMEM((tm, D), BF16)],
        compiler_params=pltpu.CompilerParams(vmem_limit_bytes=VMEM_LIMIT_BYTES),
        name="in_projection",
    )(x, n1g, w_in, qg, kg)


def _split3(x):
    hi = x.astype(BF16).astype(F32)
    r = x - hi
    mid = r.astype(BF16).astype(F32)
    lo = (r - mid).astype(BF16).astype(F32)
    return hi, mid, lo


def _alibi_constants(slopes, seq, tq):
    parts = np.stack(_split3(np.asarray(slopes, np.float32)), axis=0)
    idx = np.arange(LANES, dtype=np.int32)
    part_of = parts[idx % 3].T
    kpos = np.arange(seq, dtype=np.int32)[None, :, None]
    a = idx[None, None, :]
    kcol = np.where(a < 3, -1.0, np.where(a < 9, -part_of[:, None, :],
           np.where(a < 12, ((kpos // POS_SPLIT) * POS_SPLIT).astype(np.float32),
           np.where(a < N_AUG, (kpos % POS_SPLIT).astype(np.float32), 0.0))))
    qloc = (np.arange(2 * tq, dtype=np.int32) % tq)[None, None, :]
    a = np.arange(BF16_SUBLANES, dtype=np.int32)[None, :, None]
    qrow = np.where((a >= 3) & (a < 6), ((qloc // POS_SPLIT) * POS_SPLIT).astype(np.float32),
           np.where((a >= 6) & (a < 9), (qloc % POS_SPLIT).astype(np.float32),
           np.where((a >= 9) & (a < N_AUG), part_of[:, :BF16_SUBLANES, None], 0.0)))
    return kcol.astype(BF16), qrow.astype(np.float32)


def _attn_kernel(slopes_ref, qt_ref, k_ref, kcol_ref, qrow_ref, dist_ref, vt_ref,
                 lq1_ref, lk1_ref, lq2_ref, lk2_ref, sg_ref, *rest, n_weights):
    w_f32 = rest[:n_weights]
    o_ref = rest[n_weights]
    w_bf16 = rest[n_weights + 1:2 * n_weights + 1]
    qmain_ref, qaug_ref, acc_ref, kmax_ref, fast_ref = rest[2 * n_weights + 1:]
    for src, dst in zip(w_f32, w_bf16):
        dst[...] = src[...].astype(BF16)

    h = pl.program_id(1)
    n_q, dv, tq = qt_ref.shape
    n_kc, _, tk = vt_ref.shape
    slope = slopes_ref[h]

    def rows_of(index, size):
        start = index * size
        return pl.ds(start if isinstance(start, int) else pl.multiple_of(start, size), size)

    def k_chunk(j):
        return k_ref[rows_of(j, tk), :]

    @pl.when((pl.program_id(0) == 0) & (h == 0))
    def _():
        qaug_ref[:, :, BF16_SUBLANES:, :] = jnp.zeros(
            (qaug_ref.shape[0], 3, LANES - BF16_SUBLANES, 2 * tq), BF16)

    ones_bd = _group_ones(LANES)
    best = jnp.zeros((1, LANES), F32)
    for c in range(n_kc):
        kf = k_ref[c * tk:(c + 1) * tk, :].astype(F32)
        best = jnp.maximum(best, jnp.max(_dot((kf * kf).astype(BF16), ones_bd), axis=0, keepdims=True))
    kmax_ref[0:1, :] = jnp.sqrt(best) * KMAX_MARGIN
    lam = (jnp.exp(jnp.sum(lq1_ref[...] * lk1_ref[...], axis=-1, keepdims=True))
           - jnp.exp(jnp.sum(lq2_ref[...] * lk2_ref[...], axis=-1, keepdims=True)) + LAMBDA_INIT)

    def prepare(blk, slot):
        qt = qt_ref[blk]
        sub = lax.broadcasted_iota(jnp.int32, (dv, tq), 0)
        zero = jnp.zeros_like(qt)
        qmain_ref[slot, :, 0:tq] = jnp.where(sub < DA_HEAD_DIM, qt, zero)
        qmain_ref[slot, :, tq:2 * tq] = jnp.where(sub < DA_HEAD_DIM, zero, qt)
        qsq = qt.astype(F32) * qt.astype(F32)
        n1 = jnp.sqrt(jnp.sum(qsq[:DA_HEAD_DIM], axis=0, keepdims=True))
        n2 = jnp.sqrt(jnp.sum(qsq[DA_HEAD_DIM:], axis=0, keepdims=True))
        m = jnp.concatenate([n1 * kmax_ref[0:1, 0:1], n2 * kmax_ref[0:1, DA_HEAD_DIM:DA_HEAD_DIM + 1]], axis=1)
        a = lax.broadcasted_iota(jnp.int32, (BF16_SUBLANES, 2 * tq), 0)
        positional = qrow_ref[...] + jnp.where((a >= 3) & (a < 6), jnp.asarray(blk * tq, F32), 0.0)
        m_hi, m_mid, m_lo = _split3(m)
        m_rows = jnp.where(a == 0, m_hi, jnp.where(a == 1, m_mid, m_lo))
        for variant, tail in enumerate((positional, -positional, jnp.zeros_like(positional))):
            qaug_ref[slot, variant, 0:BF16_SUBLANES, :] = jnp.where(a < 3, m_rows, tail).astype(BF16)
        fast_ref[slot] = jnp.where(jnp.max(m) <= SAFE_EXP2_RANGE, 1, 0)

    def finish(blk, slot):
        o = acc_ref[slot, 0:dv, :] / acc_ref[slot, dv:dv + 1, :]
        o = o[:, :tq] - lam * o[:, tq:]
        ms = jnp.mean(o * o, axis=0, keepdims=True)
        o = (o * lax.rsqrt(ms + EPS)).T
        o_ref[rows_of(blk, tq), :] = (o * sg_ref[...] * (1.0 - LAMBDA_INIT)).astype(BF16)

    def scores_fast(blk, slot):
        ones_row = jnp.where(lax.broadcasted_iota(jnp.int32, (BF16_SUBLANES, tk), 0) == 0, 1.0, 0.0).astype(BF16)

        def chunk(j, variant, bias):
            k_aug = jnp.concatenate([k_chunk(j), kcol_ref[rows_of(j, tk), :]], axis=1)
            e = _dot(k_aug, jnp.concatenate([qmain_ref[slot], qaug_ref[slot, variant]], axis=0))
            if bias is not None:
                e = e - bias
            p = jnp.exp2(e).astype(BF16)
            return _dot(jnp.concatenate([vt_ref[j], ones_row], axis=0), p)

        diag_bias = dist_ref[...] * slope
        acc = chunk(blk, 2, jnp.concatenate([diag_bias, diag_bias], axis=1))
        for t in range(n_kc - 1):
            j = t + jnp.where(t >= blk, 1, 0)
            acc = acc + chunk(j, jnp.where(j > blk, 1, 0), None)
        acc_ref[slot] = acc

    def scores_online(blk, slot):
        signed =(lax.broadcasted_iota(jnp.int32, (tk, 2 * tq), 1) % tq
                  - lax.broadcasted_iota(jnp.int32, (tk, 2 * tq), 0)).astype(F32) * slope

        def body(j, carry):
            mx, l, acc = carry
            off = jnp.asarray(blk * tq - j * tk, F32) * slope
            s = _dot(k_chunk(j), qmain_ref[slot]) - jnp.abs(signed + off)
            m_new = jnp.maximum(mx, jnp.max(s, axis=0, keepdims=True))
            alpha = jnp.exp2(mx - m_new)
            p = jnp.exp2(s - m_new)
            l = alpha * l + jnp.sum(p, axis=0, keepdims=True)
            acc = alpha * acc + _dot(vt_ref[j], p.astype(BF16))
            return m_new, l, acc

        m0 = jnp.full((1, 2 * tq), -jnp.inf, F32)
        l0 = jnp.zeros((1, 2 * tq), F32)
        a0 = jnp.zeros((dv, 2 * tq), F32)
        _, l, acc = lax.fori_loop(0, n_kc, body, (m0, l0, a0))
        acc_ref[slot, 0:dv, :] = acc
        acc_ref[slot, dv:dv + BF16_SUBLANES, :] = jnp.broadcast_to(l, (BF16_SUBLANES, 2 * tq))

    group = BLOCKS_PER_REGION

    def region(first_blk, cur, oth):
        blks = [first_blk + g for g in range(group)]
        prev = [jnp.maximum(b - group, g) for g, b in enumerate(blks)]
        nxt = [jnp.minimum(b + group, n_q - group + g) for g, b in enumerate(blks)]
        fast = fast_ref[cur[0]] == 1
        for s in cur[1:]:
            fast = fast & (fast_ref[s] == 1)

        def body(scores):
            for g in range(group):
                finish(prev[g], oth[g])
            for g in range(group):
                scores(blks[g], cur[g])
            for g in range(group):
                prepare(nxt[g], oth[g])

        pl.when(fast)(lambda: body(scores_fast))
        pl.when(jnp.logical_not(fast))(lambda: body(scores_online))

    slots_a = tuple(range(group))
    slots_b = tuple(range(group, 2 * group))
    for g in range(group):
        acc_ref[slots_b[g]] = jnp.ones(acc_ref.shape[1:], F32)
        prepare(g, slots_a[g])

    def two_regions(u, carry):
        region(2 * group * u, slots_a, slots_b)
        region(2 * group * u + group, slots_b, slots_a)
        return carry

    lax.fori_loop(0, n_q // (2 * group), two_regions, 0)
    for g in range(group):
        finish(n_q - group + g, slots_b[g])


def _attention(slopes, qt, k, vt, lq1, lk1, lq2, lk2, subln_g, later_weights):
    B, H, n_q, dv, tq = qt.shape
    n_kc = vt.shape[2]
    S = k.shape[2]
    n_slots = 2 * BLOCKS_PER_REGION
    assert n_q % n_slots == 0
    steps = B * H
    slabs = [w.reshape(steps, w.shape[0] // steps, w.shape[1]) for w in later_weights]
    slab_spec = lambda s: pl.BlockSpec((None,) + s.shape[1:], lambda b, h, *_: (b * H + h, 0, 0))
    kcol, qrow = _alibi_constants(slopes, S, tq)
    dist = np.abs(np.arange(TK, dtype=np.int32)[:, None]
                  - np.arange(tq, dtype=np.int32)[None, :]).astype(np.float32)
    slopes = np.asarray(slopes, np.float32)
    small = lambda n: pl.BlockSpec((1, n), lambda b, h, *_: (0, 0))
    grid_spec = pltpu.PrefetchScalarGridSpec(
        num_scalar_prefetch=1,
        grid=(B, H),
        in_specs=[
            pl.BlockSpec((None, None, n_q, dv, tq), lambda b, h, *_: (b, h, 0, 0, 0)),
            pl.BlockSpec((None, None, S, dv), lambda b, h, *_: (b, h, 0, 0)),
            pl.BlockSpec((None, S, LANES), lambda b, h, *_: (h, 0, 0)),
            pl.BlockSpec((None, BF16_SUBLANES, 2 * tq), lambda b, h, *_: (h, 0, 0)),
            pl.BlockSpec((TK, tq), lambda b, h, *_: (0, 0)),
            pl.BlockSpec((None, None, n_kc, dv, TK), lambda b, h, *_: (b, h, 0, 0, 0)),
            small(DA_HEAD_DIM), small(DA_HEAD_DIM), small(DA_HEAD_DIM), small(DA_HEAD_DIM),
            small(DA_V_DIM),
        ] + [slab_spec(s) for s in slabs],
        out_specs=[pl.BlockSpec((None, S, dv), lambda b, h, *_: (b, 0, h))] + [slab_spec(s) for s in slabs],
        scratch_shapes=[
            pltpu.VMEM((n_slots, dv, 2 * tq), BF16),
            pltpu.VMEM((n_slots, 3, LANES, 2 * tq), BF16),
            pltpu.VMEM((n_slots, dv + BF16_SUBLANES, 2 * tq), F32),
            pltpu.VMEM((8, LANES), F32),
            pltpu.SMEM((n_slots,), jnp.int32),
        ],
    )
    o_da, *cast = pl.pallas_call(
        functools.partial(_attn_kernel, n_weights=len(slabs)),
        grid_spec=grid_spec,
        out_shape=[jax.ShapeDtypeStruct((B, S, H * dv), BF16)]
                  + [jax.ShapeDtypeStruct(s.shape, BF16) for s in slabs],
        compiler_params=pltpu.CompilerParams(vmem_limit_bytes=VMEM_LIMIT_BYTES,
                                             dimension_semantics=("arbitrary", "arbitrary")),
        name="diff_attention",
    )(slopes, qt, k, kcol, qrow, dist, vt, lq1, lk1, lq2, lk2, subln_g, *slabs)
    return o_da, [c.reshape(w.shape) for c, w in zip(cast, later_weights)]


def _merge_kernel(x_ref, uv_ref, oda_ref, gate_ref, bg_ref, lng_ref, lnb_ref, sw_ref, sb_ref,
                  psg_ref, pda_ref, wo_ref, h_ref, vn_ref):
    tm = x_ref.shape[0]
    n_chunks = tm // CHUNK

    y_da = _dot(oda_ref[...], pda_ref[...])

    gu = jax.nn.gelu(uv_ref[:, :SG_WIDTH].astype(F32))
    gv = jax.nn.gelu(uv_ref[:, SG_WIDTH:].astype(F32))
    mu = jnp.mean(gv, axis=-1, keepdims=True)
    cen = gv - mu
    var = jnp.mean(cen * cen, axis=-1, keepdims=True)
    vn_ref[...] = (cen * lax.rsqrt(var + EPS) * lng_ref[...] + lnb_ref[...]).astype(BF16)

    lane = lax.broadcasted_iota(jnp.int32, (CHUNK, tm), 1)
    first_group = (lane % LANES) < SG_GROUP_DIM
    mixed = []
    for pr in range(SG_GROUPS // 2):
        vp = jnp.concatenate([vn_ref[c * CHUNK:(c + 1) * CHUNK, pr * LANES:(pr + 1) * LANES]
                              for c in range(n_chunks)], axis=1)
        mixed.append(jnp.where(first_group, _dot(sw_ref[2 * pr], vp), _dot(sw_ref[2 * pr + 1], vp)))
    bias = sb_ref[...]
    sv = jnp.concatenate(
        [jnp.concatenate([m[:, c * LANES:(c + 1) * LANES] for m in mixed], axis=1) + bias
         for c in range(n_chunks)], axis=0)
    o_sg = (gu * sv).astype(BF16)

    y_sg = _dot(o_sg, psg_ref[...])
    g_sg = jax.nn.sigmoid(gate_ref[:, :D_MODEL].astype(F32) + bg_ref[:, :D_MODEL])
    g_da = jax.nn.sigmoid(gate_ref[:, D_MODEL:].astype(F32) + bg_ref[:, D_MODEL:])
    z = (g_sg * y_sg + g_da * y_da).astype(BF16)
    h_ref[...] = x_ref[...] + _dot(z, wo_ref[...])


def _merge(x, uv, o_da, gate_pre, b_gate, ln_g, ln_b, sg_w, sg_bias, p_sg, p_da, w_out):
    B, S, D = x.shape
    tm = TM_MERGE
    tok = lambda width: pl.BlockSpec((None, tm, width), lambda b, i: (b, i, 0))
    return pl.pallas_call(
        _merge_kernel,
        grid=(B, S // tm),
        in_specs=[
            tok(D), tok(2 * SG_WIDTH), tok(DA_WIDTH), tok(2 * D_MODEL),
            _const_spec(b_gate.shape), _const_spec(ln_g.shape), _const_spec(ln_b.shape),
            _const_spec(sg_w.shape), _const_spec(sg_bias.shape),
            _const_spec(p_sg.shape), _const_spec(p_da.shape), _const_spec(w_out.shape),
        ],
        out_specs=tok(D),
        out_shape=jax.ShapeDtypeStruct((B, S, D), F32),
        scratch_shapes=[pltpu.VMEM((tm, SG_WIDTH), BF16)],
        compiler_params=pltpu.CompilerParams(vmem_limit_bytes=VMEM_LIMIT_BYTES),
        name="merge",
    )(x, uv, o_da, gate_pre, b_gate, ln_g, ln_b, sg_w, sg_bias, p_sg, p_da, w_out)


def _ffn_kernel(h_ref, g_ref, wg_ref, wu_ref, wd_ref, o_ref):
    hres = h_ref[...]
    ms = jnp.mean(hres * hres, axis=-1, keepdims=True)
    hn = (hres * lax.rsqrt(ms + EPS) * g_ref[...]).astype(BF16)
    d_ff = wg_ref.shape[1]
    acc = hres
    for c in range(d_ff // FF_CHUNK):
        sl = slice(c * FF_CHUNK, (c + 1) * FF_CHUNK)
        a = jax.nn.silu(_dot(hn, wg_ref[:, sl])) * _dot(hn, wu_ref[:, sl])
        acc = acc + _dot(a.astype(BF16), wd_ref[sl, :])
    o_ref[...] = acc


def _ffn(h, g, w_gate, w_up, w_down):
    B, S, D = h.shape
    tm = TM_FFN
    tok = pl.BlockSpec((None, tm, D), lambda b, i: (b, i, 0))
    return pl.pallas_call(
        _ffn_kernel,
        grid=(B, S // tm),
        in_specs=[tok, _const_spec((1, D)), _const_spec(w_gate.shape), _const_spec(w_up.shape),
                  _const_spec(w_down.shape)],
        out_specs=tok,
        out_shape=jax.ShapeDtypeStruct((B, S, D), F32),
        compiler_params=pltpu.CompilerParams(vmem_limit_bytes=VMEM_LIMIT_BYTES),
        name="swiglu_ffn",
    )(h, g, w_gate, w_up, w_down)


def kernel(x, norm1_g, w_in, b_gate, sg_ln_g, sg_ln_b, sg_w, sg_b, q_norm_g, k_norm_g, lam_q1, lam_k1, lam_q2, lam_k2, subln_g, w_proj_sg, w_proj_da, w_out, norm2_g, w_ffn_gate, w_ffn_up, w_ffn_down):
    depth = w_in.shape[0]
    row = lambda a: a.reshape(1, -1).astype(F32)
    n_rep = DA_WIDTH // DA_HEAD_DIM
    slopes = [2.0 ** (-8.0 * (i + 1) / DA_HEADS) * LOG2E for i in range(DA_HEADS)]
    for l in range(depth):
        qg = row(jnp.tile(q_norm_g[l], n_rep)) * (LOG2E / math.sqrt(DA_HEAD_DIM))
        kg = row(jnp.tile(k_norm_g[l], n_rep))
        uv, qt, k, vt, gate_pre = _in_projection(x, row(norm1_g[l]), w_in[l].astype(BF16), qg, kg)
        o_da, (p_sg, p_da, w_o, w_gate, w_up, w_down) = _attention(
            slopes, qt, k, vt, row(lam_q1[l]), row(lam_k1[l]), row(lam_q2[l]), row(lam_k2[l]),
            row(subln_g[l]),
            [w_proj_sg[l], w_proj_da[l], w_out[l], w_ffn_gate[l], w_ffn_up[l], w_ffn_down[l]])
        sg_bias = jnp.repeat(sg_b[l].T.astype(F32), SG_GROUP_DIM, axis=1)
        h = _merge(x, uv, o_da, gate_pre, row(b_gate[l]), row(sg_ln_g[l]), row(sg_ln_b[l]),
                   sg_w[l].astype(BF16), sg_bias, p_sg, p_da, w_o)
        x = _ffn(h, row(norm2_g[l]), w_gate, w_up, w_down)
    return x
```

```python
import functools
import math

import numpy as np
import jax
import jax.numpy as jnp
from jax import lax
from jax.experimental import pallas as pl
from jax.experimental.pallas import tpu as pltpu

D_MODEL = 1024
SG_GROUPS = 8
SG_GROUP_DIM = 64
SG_WIDTH = SG_GROUPS * SG_GROUP_DIM
CHUNK = 128
DA_HEADS = 8
DA_HEAD_DIM = 64
DA_V_DIM = 2 * DA_HEAD_DIM
DA_WIDTH = DA_HEADS * DA_V_DIM
EPS = 1e-6
LAMBDA_INIT = 0.8 - 0.6 * math.exp(-0.3 * 0)
LOG2E = math.log2(math.e)

BF16 = jnp.bfloat16
F32 = jnp.float32

LANES = 128
BF16_SUBLANES = 16
MXU_DIM = 256
VMEM_LIMIT_BYTES = 56 * 1024 * 1024

TM_PROJ = 512
TQ = 512
TK = 512
POS_SPLIT = 64
N_AUG = 15
SAFE_EXP2_RANGE = 60.0
KMAX_MARGIN = 1.01
BLOCKS_PER_REGION = 2
assert TQ == TK and N_AUG <= BF16_SUBLANES
TM_MERGE = 512
TM_FFN = 1024
FF_CHUNK = 256


def _dot(a, b):
    return jnp.dot(a, b, preferred_element_type=F32)


def _const_spec(shape):
    n = len(shape)
    return pl.BlockSpec(shape, lambda *_: (0,) * n, pipeline_mode=pl.Buffered(1))


def _group_ones(n, value=1.0):
    r = lax.broadcasted_iota(jnp.int32, (n, n), 0) // DA_HEAD_DIM
    c = lax.broadcasted_iota(jnp.int32, (n, n), 1) // DA_HEAD_DIM
    return jnp.where(r == c, value, 0.0).astype(BF16)


def _inproj_kernel(x_ref, n1g_ref, w_ref, qg_ref, kg_ref,
                   uv_ref, qt_ref, k_ref, vt_ref, gate_ref, xn_ref):
    x = x_ref[...]
    ms = jnp.mean(x * x, axis=-1, keepdims=True)
    xn_ref[...] = (x * lax.rsqrt(ms + EPS) * n1g_ref[...]).astype(BF16)
    averager = _group_ones(MXU_DIM, 1.0 / DA_HEAD_DIM)

    def unit_rms(p, gain_ref):
        sq = (p * p).astype(BF16)
        mean_sq = jnp.concatenate([_dot(sq[:, i * MXU_DIM:(i + 1) * MXU_DIM], averager)
                                   for i in range(p.shape[1] // MXU_DIM)], axis=1)
        return p * lax.rsqrt(mean_sq + EPS) * gain_ref[...]

    def spatial(p):
        uv_ref[...] = p.astype(BF16)

    def query(p):
        tm = p.shape[0]
        first = lax.broadcasted_iota(jnp.int32, (DA_V_DIM, tm), 0) < DA_HEAD_DIM
        gain = jnp.concatenate([qg_ref[...]] * (tm // LANES), axis=1)
        for h in range(DA_HEADS):
            pt = p[:, h * DA_V_DIM:(h + 1) * DA_V_DIM].T
            sq = pt * pt
            r1 = lax.rsqrt(jnp.mean(sq[:DA_HEAD_DIM], axis=0, keepdims=True) + EPS)
            r2 = lax.rsqrt(jnp.mean(sq[DA_HEAD_DIM:], axis=0, keepdims=True) + EPS)
            qt_ref[h] = (pt * jnp.where(first, r1, r2) * gain).astype(BF16)

    def key(p):
        kn = unit_rms(p, kg_ref).astype(BF16)
        for h in range(DA_HEADS):
            k_ref[h] = kn[:, h * DA_V_DIM:(h + 1) * DA_V_DIM]

    def value(p):
        for h in range(DA_HEADS):
            vt_ref[h] = p[:, h * DA_V_DIM:(h + 1) * DA_V_DIM].T.astype(BF16)

    def gate_sg(p):
        gate_ref[:, :D_MODEL] = p.astype(BF16)

    def gate_da(p):
        gate_ref[:, D_MODEL:] = p.astype(BF16)

    col = 0
    for width, epilogue in ((2 * SG_WIDTH, spatial), (DA_WIDTH, query), (DA_WIDTH, key), (DA_WIDTH, value),
                            (D_MODEL, gate_sg), (D_MODEL, gate_da)):
        epilogue(_dot(xn_ref[...], w_ref[:, col:col + width]))
        col += width


def _in_projection(x, n1g, w_in, qg, kg):
    B, S, D = x.shape
    tm = TM_PROJ
    n_t = S // tm
    n_kc = S // TK
    sub = TK // tm
    in_cols = w_in.shape[1]
    tok = lambda width: pl.BlockSpec((None, tm, width), lambda b, i: (b, i, 0))
    out_shape = (
        jax.ShapeDtypeStruct((B, S, 2 * SG_WIDTH), BF16),
        jax.ShapeDtypeStruct((B, DA_HEADS, S // TQ, DA_V_DIM, TQ), BF16),
        jax.ShapeDtypeStruct((B, DA_HEADS, S, DA_V_DIM), BF16),
        jax.ShapeDtypeStruct((B, DA_HEADS, n_kc, DA_V_DIM, TK), BF16),
        jax.ShapeDtypeStruct((B, S, 2 * D_MODEL), BF16),
    )
    out_specs = (
        tok(2 * SG_WIDTH),
        pl.BlockSpec((None, DA_HEADS, None, DA_V_DIM, tm), lambda b, i: (b, 0, i // sub, 0, i % sub)),
        pl.BlockSpec((None, DA_HEADS, tm, DA_V_DIM), lambda b, i: (b, 0, i, 0)),
        pl.BlockSpec((None, DA_HEADS, None, DA_V_DIM, tm), lambda b, i: (b, 0, i // sub, 0, i % sub)),
        tok(2 * D_MODEL),
    )
    in_specs = [
        tok(D),
        _const_spec((1, D)),
        _const_spec((D, in_cols)),
        _const_spec((DA_V_DIM, LANES)),
        _const_spec((1, DA_WIDTH)),
    ]
    return pl.pallas_call(
        _inproj_kernel,
        grid=(B, n_t),
        in_specs=in_specs,
        out_specs=out_specs,
        out_shape=out_shape,
        scratch_shapes=[pltpu.VMEM((tm, D), BF16)],
        compiler_params=pltpu.CompilerParams(vmem_limit_bytes=VMEM_LIMIT_BYTES),
        name="in_projection",
    )(x, n1g, w_in, qg, kg)


def _split3(x):
    hi = x.astype(BF16).astype(F32)
    r = x - hi
    mid = r.astype(BF16).astype(F32)
    lo = (r - mid).astype(BF16).astype(F32)
    return hi, mid, lo


def _alibi_constants(slopes, seq, tq):
    parts = np.stack(_split3(np.asarray(slopes, np.float32)), axis=0)
    idx = np.arange(LANES, dtype=np.int32)
    part_of = parts[idx % 3].T
    kpos = np.arange(seq, dtype=np.int32)[None, :, None]
    a = idx[None, None, :]
    kcol = np.where(a < 3, -1.0, np.where(a < 9, -part_of[:, None, :],
           np.where(a < 12, ((kpos // POS_SPLIT) * POS_SPLIT).astype(np.float32),
           np.where(a < N_AUG, (kpos % POS_SPLIT).astype(np.float32), 0.0))))
    qloc = (np.arange(2 * tq, dtype=np.int32) % tq)[None, None, :]
    a = np.arange(BF16_SUBLANES, dtype=np.int32)[None, :, None]
    qrow = np.where((a >= 3) & (a < 6), ((qloc // POS_SPLIT) * POS_SPLIT).astype(np.float32),
           np.where((a >= 6) & (a < 9), (qloc % POS_SPLIT).astype(np.float32),
           np.where((a >= 9) & (a < N_AUG), part_of[:, :BF16_SUBLANES, None], 0.0)))
    return kcol.astype(BF16), qrow.astype(np.float32)


def _attn_kernel(slopes_ref, qt_ref, k_ref, kcol_ref, qrow_ref, dist_ref, vt_ref,
                 lq1_ref, lk1_ref, lq2_ref, lk2_ref, sg_ref, *rest, n_weights):
    w_f32 = rest[:n_weights]
    o_ref = rest[n_weights]
    w_bf16 = rest[n_weights + 1:2 * n_weights + 1]
    qmain_ref, qaug_ref, acc_ref, kmax_ref, fast_ref = rest[2 * n_weights + 1:]
    for src, dst in zip(w_f32, w_bf16):
        dst[...] = src[...].astype(BF16)

    h = pl.program_id(1)
    n_q, dv, tq = qt_ref.shape
    n_kc, _, tk = vt_ref.shape
    slope = slopes_ref[h]

    def rows_of(index, size):
        start = index * size
        return pl.ds(start if isinstance(start, int) else pl.multiple_of(start, size), size)

    def k_chunk(j):
        return k_ref[rows_of(j, tk), :]

    @pl.when((pl.program_id(0) == 0) & (h == 0))
    def _():
        qaug_ref[:, :, BF16_SUBLANES:, :] = jnp.zeros(
            (qaug_ref.shape[0], 3, LANES - BF16_SUBLANES, 2 * tq), BF16)

    ones_bd = _group_ones(LANES)
    best = jnp.zeros((1, LANES), F32)
    for c in range(n_kc):
        kf = k_ref[c * tk:(c + 1) * tk, :].astype(F32)
        best = jnp.maximum(best, jnp.max(_dot((kf * kf).astype(BF16), ones_bd), axis=0, keepdims=True))
    kmax_ref[0:1, :] = jnp.sqrt(best) * KMAX_MARGIN
    lam = (jnp.exp(jnp.sum(lq1_ref[...] * lk1_ref[...], axis=-1, keepdims=True))
           - jnp.exp(jnp.sum(lq2_ref[...] * lk2_ref[...], axis=-1, keepdims=True)) + LAMBDA_INIT)

    def prepare(blk, slot):
        qt = qt_ref[blk]
        sub = lax.broadcasted_iota(jnp.int32, (dv, tq), 0)
        zero = jnp.zeros_like(qt)
        qmain_ref[slot, :, 0:tq] = jnp.where(sub < DA_HEAD_DIM, qt, zero)
        qmain_ref[slot, :, tq:2 * tq] = jnp.where(sub < DA_HEAD_DIM, zero, qt)
        qsq = qt.astype(F32) * qt.astype(F32)
        n1 = jnp.sqrt(jnp.sum(qsq[:DA_HEAD_DIM], axis=0, keepdims=True))
        n2 = jnp.sqrt(jnp.sum(qsq[DA_HEAD_DIM:], axis=0, keepdims=True))
        m = jnp.concatenate([n1 * kmax_ref[0:1, 0:1], n2 * kmax_ref[0:1, DA_HEAD_DIM:DA_HEAD_DIM + 1]], axis=1)
        a = lax.broadcasted_iota(jnp.int32, (BF16_SUBLANES, 2 * tq), 0)
        positional = qrow_ref[...] + jnp.where((a >= 3) & (a < 6), jnp.asarray(blk * tq, F32), 0.0)
        m_hi, m_mid, m_lo = _split3(m)
        m_rows = jnp.where(a == 0, m_hi, jnp.where(a == 1, m_mid, m_lo))
        for variant, tail in enumerate((positional, -positional, jnp.zeros_like(positional))):
            qaug_ref[slot, variant, 0:BF16_SUBLANES, :] = jnp.where(a < 3, m_rows, tail).astype(BF16)
        fast_ref[slot] = jnp.where(jnp.max(m) <= SAFE_EXP2_RANGE, 1, 0)

    def finish(blk, slot):
        o = acc_ref[slot, 0:dv, :] / acc_ref[slot, dv:dv + 1, :]
        o = o[:, :tq] - lam * o[:, tq:]
        ms = jnp.mean(o * o, axis=0, keepdims=True)
        o = (o * lax.rsqrt(ms + EPS)).T
        o_ref[rows_of(blk, tq), :] = (o * sg_ref[...] * (1.0 - LAMBDA_INIT)).astype(BF16)

    def scores_fast(blk, slot):
        ones_row = jnp.where(lax.broadcasted_iota(jnp.int32, (BF16_SUBLANES, tk), 0) == 0, 1.0, 0.0).astype(BF16)

        def chunk(j, variant, bias):
            k_aug = jnp.concatenate([k_chunk(j), kcol_ref[rows_of(j, tk), :]], axis=1)
            e = _dot(k_aug, jnp.concatenate([qmain_ref[slot], qaug_ref[slot, variant]], axis=0))
            if bias is not None:
                e = e - bias
            p = jnp.exp2(e).astype(BF16)
            return _dot(jnp.concatenate([vt_ref[j], ones_row], axis=0), p)

        diag_bias = dist_ref[...] * slope
        acc = chunk(blk, 2, jnp.concatenate([diag_bias, diag_bias], axis=1))
        for t in range(n_kc - 1):
            j = t + jnp.where(t >= blk, 1, 0)
            acc = acc + chunk(j, jnp.where(j > blk, 1, 0), None)
        acc_ref[slot] = acc

    def scores_online(blk, slot):
        signed =(lax.broadcasted_iota(jnp.int32, (tk, 2 * tq), 1) % tq
                  - lax.broadcasted_iota(jnp.int32, (tk, 2 * tq), 0)).astype(F32) * slope

        def body(j, carry):
            mx, l, acc = carry
            off = jnp.asarray(blk * tq - j * tk, F32) * slope
            s = _dot(k_chunk(j), qmain_ref[slot]) - jnp.abs(signed + off)
            m_new = jnp.maximum(mx, jnp.max(s, axis=0, keepdims=True))
            alpha = jnp.exp2(mx - m_new)
            p = jnp.exp2(s - m_new)
            l = alpha * l + jnp.sum(p, axis=0, keepdims=True)
            acc = alpha * acc + _dot(vt_ref[j], p.astype(BF16))
            return m_new, l, acc

        m0 = jnp.full((1, 2 * tq), -jnp.inf, F32)
        l0 = jnp.zeros((1, 2 * tq), F32)
        a0 = jnp.zeros((dv, 2 * tq), F32)
        _, l, acc = lax.fori_loop(0, n_kc, body, (m0, l0, a0))
        acc_ref[slot, 0:dv, :] = acc
        acc_ref[slot, dv:dv + BF16_SUBLANES, :] = jnp.broadcast_to(l, (BF16_SUBLANES, 2 * tq))

    group = BLOCKS_PER_REGION

    def region(first_blk, cur, oth):
        blks = [first_blk + g for g in range(group)]
        prev = [jnp.maximum(b - group, g) for g, b in enumerate(blks)]
        nxt = [jnp.minimum(b + group, n_q - group + g) for g, b in enumerate(blks)]
        fast = fast_ref[cur[0]] == 1
        for s in cur[1:]:
            fast = fast & (fast_ref[s] == 1)

        def body(scores):
            for g in range(group):
                finish(prev[g], oth[g])
            for g in range(group):
                scores(blks[g], cur[g])
            for g in range(group):
                prepare(nxt[g], oth[g])

        pl.when(fast)(lambda: body(scores_fast))
        pl.when(jnp.logical_not(fast))(lambda: body(scores_online))

    slots_a = tuple(range(group))
    slots_b = tuple(range(group, 2 * group))
    for g in range(group):
        acc_ref[slots_b[g]] = jnp.ones(acc_ref.shape[1:], F32)
        prepare(g, slots_a[g])

    def two_regions(u, carry):
        region(2 * group * u, slots_a, slots_b)
        region(2 * group * u + group, slots_b, slots_a)
        return carry

    lax.fori_loop(0, n_q // (2 * group), two_regions, 0)
    for g in range(group):
        finish(n_q - group + g, slots_b[g])


def _attention(slopes, qt, k, vt, lq1, lk1, lq2, lk2, subln_g, later_weights):
    B, H, n_q, dv, tq = qt.shape
    n_kc = vt.shape[2]
    S = k.shape[2]
    n_slots = 2 * BLOCKS_PER_REGION
    assert n_q % n_slots == 0
    steps = B * H
    slabs = [w.reshape(steps, w.shape[0] // steps, w.shape[1]) for w in later_weights]
    slab_spec = lambda s: pl.BlockSpec((None,) + s.shape[1:], lambda b, h, *_: (b * H + h, 0, 0))
    kcol, qrow = _alibi_constants(slopes, S, tq)
    dist = np.abs(np.arange(TK, dtype=np.int32)[:, None]
                  - np.arange(tq, dtype=np.int32)[None, :]).astype(np.float32)
    slopes = np.asarray(slopes, np.float32)
    small = lambda n: pl.BlockSpec((1, n), lambda b, h, *_: (0, 0))
    grid_spec = pltpu.PrefetchScalarGridSpec(
        num_scalar_prefetch=1,
        grid=(B, H),
        in_specs=[
            pl.BlockSpec((None, None, n_q, dv, tq), lambda b, h, *_: (b, h, 0, 0, 0)),
            pl.BlockSpec((None, None, S, dv), lambda b, h, *_: (b, h, 0, 0)),
            pl.BlockSpec((None, S, LANES), lambda b, h, *_: (h, 0, 0)),
            pl.BlockSpec((None, BF16_SUBLANES, 2 * tq), lambda b, h, *_: (h, 0, 0)),
            pl.BlockSpec((TK, tq), lambda b, h, *_: (0, 0)),
            pl.BlockSpec((None, None, n_kc, dv, TK), lambda b, h, *_: (b, h, 0, 0, 0)),
            small(DA_HEAD_DIM), small(DA_HEAD_DIM), small(DA_HEAD_DIM), small(DA_HEAD_DIM),
            small(DA_V_DIM),
        ] + [slab_spec(s) for s in slabs],
        out_specs=[pl.BlockSpec((None, S, dv), lambda b, h, *_: (b, 0, h))] + [slab_spec(s) for s in slabs],
        scratch_shapes=[
            pltpu.VMEM((n_slots, dv, 2 * tq), BF16),
            pltpu.VMEM((n_slots, 3, LANES, 2 * tq), BF16),
            pltpu.VMEM((n_slots, dv + BF16_SUBLANES, 2 * tq), F32),
            pltpu.VMEM((8, LANES), F32),
            pltpu.SMEM((n_slots,), jnp.int32),
        ],
    )
    o_da, *cast = pl.pallas_call(
        functools.partial(_attn_kernel, n_weights=len(slabs)),
        grid_spec=grid_spec,
        out_shape=[jax.ShapeDtypeStruct((B, S, H * dv), BF16)]
                  + [jax.ShapeDtypeStruct(s.shape, BF16) for s in slabs],
        compiler_params=pltpu.CompilerParams(vmem_limit_bytes=VMEM_LIMIT_BYTES,
                                             dimension_semantics=("arbitrary", "arbitrary")),
        name="diff_attention",
    )(slopes, qt, k, kcol, qrow, dist, vt, lq1, lk1, lq2, lk2, subln_g, *slabs)
    return o_da, [c.reshape(w.shape) for c, w in zip(cast, later_weights)]


def _merge_kernel(x_ref, uv_ref, oda_ref, gate_ref, bg_ref, lng_ref, lnb_ref, sw_ref, sb_ref,
                  psg_ref, pda_ref, wo_ref, h_ref, vn_ref):
    tm = x_ref.shape[0]
    n_chunks = tm // CHUNK

    y_da = _dot(oda_ref[...], pda_ref[...])

    gu = jax.nn.gelu(uv_ref[:, :SG_WIDTH].astype(F32))
    gv = jax.nn.gelu(uv_ref[:, SG_WIDTH:].astype(F32))
    mu = jnp.mean(gv, axis=-1, keepdims=True)
    cen = gv - mu
    var = jnp.mean(cen * cen, axis=-1, keepdims=True)
    vn_ref[...] = (cen * lax.rsqrt(var + EPS) * lng_ref[...] + lnb_ref[...]).astype(BF16)

    lane = lax.broadcasted_iota(jnp.int32, (CHUNK, tm), 1)
    first_group = (lane % LANES) < SG_GROUP_DIM
    mixed = []
    for pr in range(SG_GROUPS // 2):
        vp = jnp.concatenate([vn_ref[c * CHUNK:(c + 1) * CHUNK, pr * LANES:(pr + 1) * LANES]
                              for c in range(n_chunks)], axis=1)
        mixed.append(jnp.where(first_group, _dot(sw_ref[2 * pr], vp), _dot(sw_ref[2 * pr + 1], vp)))
    bias = sb_ref[...]
    sv = jnp.concatenate(
        [jnp.concatenate([m[:, c * LANES:(c + 1) * LANES] for m in mixed], axis=1) + bias
         for c in range(n_chunks)], axis=0)
    o_sg = (gu * sv).astype(BF16)

    y_sg = _dot(o_sg, psg_ref[...])
    g_sg = jax.nn.sigmoid(gate_ref[:, :D_MODEL].astype(F32) + bg_ref[:, :D_MODEL])
    g_da = jax.nn.sigmoid(gate_ref[:, D_MODEL:].astype(F32) + bg_ref[:, D_MODEL:])
    z = (g_sg * y_sg + g_da * y_da).astype(BF16)
    h_ref[...] = x_ref[...] + _dot(z, wo_ref[...])


def _merge(x, uv, o_da, gate_pre, b_gate, ln_g, ln_b, sg_w, sg_bias, p_sg, p_da, w_out):
    B, S, D = x.shape
    tm = TM_MERGE
    tok = lambda width: pl.BlockSpec((None, tm, width), lambda b, i: (b, i, 0))
    return pl.pallas_call(
        _merge_kernel,
        grid=(B, S // tm),
        in_specs=[
            tok(D), tok(2 * SG_WIDTH), tok(DA_WIDTH), tok(2 * D_MODEL),
            _const_spec(b_gate.shape), _const_spec(ln_g.shape), _const_spec(ln_b.shape),
            _const_spec(sg_w.shape), _const_spec(sg_bias.shape),
            _const_spec(p_sg.shape), _const_spec(p_da.shape), _const_spec(w_out.shape),
        ],
        out_specs=tok(D),
        out_shape=jax.ShapeDtypeStruct((B, S, D), F32),
        scratch_shapes=[pltpu.VMEM((tm, SG_WIDTH), BF16)],
        compiler_params=pltpu.CompilerParams(vmem_limit_bytes=VMEM_LIMIT_BYTES),
        name="merge",
    )(x, uv, o_da, gate_pre, b_gate, ln_g, ln_b, sg_w, sg_bias, p_sg, p_da, w_out)


def _ffn_kernel(h_ref, g_ref, wg_ref, wu_ref, wd_ref, o_ref):
    hres = h_ref[...]
    ms = jnp.mean(hres * hres, axis=-1, keepdims=True)
    hn = (hres * lax.rsqrt(ms + EPS) * g_ref[...]).astype(BF16)
    d_ff = wg_ref.shape[1]
    acc = hres
    for c in range(d_ff // FF_CHUNK):
        sl = slice(c * FF_CHUNK, (c + 1) * FF_CHUNK)
        a = jax.nn.silu(_dot(hn, wg_ref[:, sl])) * _dot(hn, wu_ref[:, sl])
        acc = acc + _dot(a.astype(BF16), wd_ref[sl, :])
    o_ref[...] = acc


def _ffn(h, g, w_gate, w_up, w_down):
    B, S, D = h.shape
    tm = TM_FFN
    tok = pl.BlockSpec((None, tm, D), lambda b, i: (b, i, 0))
    return pl.pallas_call(
        _ffn_kernel,
        grid=(B, S // tm),
        in_specs=[tok, _const_spec((1, D)), _const_spec(w_gate.shape), _const_spec(w_up.shape),
                  _const_spec(w_down.shape)],
        out_specs=tok,
        out_shape=jax.ShapeDtypeStruct((B, S, D), F32),
        compiler_params=pltpu.CompilerParams(vmem_limit_bytes=VMEM_LIMIT_BYTES),
        name="swiglu_ffn",
    )(h, g, w_gate, w_up, w_down)


def kernel(x, norm1_g, w_in, b_gate, sg_ln_g, sg_ln_b, sg_w, sg_b, q_norm_g, k_norm_g, lam_q1, lam_k1, lam_q2, lam_k2, subln_g, w_proj_sg, w_proj_da, w_out, norm2_g, w_ffn_gate, w_ffn_up, w_ffn_down):
    depth = w_in.shape[0]
    row = lambda a: a.reshape(1, -1).astype(F32)
    n_rep = DA_WIDTH // DA_HEAD_DIM
    slopes = [2.0 ** (-8.0 * (i + 1) / DA_HEADS) * LOG2E for i in range(DA_HEADS)]
    for l in range(depth):
        qg = jnp.broadcast_to(jnp.tile(q_norm_g[l].astype(F32), 2)[:, None] * (LOG2E / math.sqrt(DA_HEAD_DIM)),
                              (DA_V_DIM, LANES))
        kg = row(jnp.tile(k_norm_g[l], n_rep))
        uv, qt, k, vt, gate_pre = _in_projection(x, row(norm1_g[l]), w_in[l].astype(BF16), qg, kg)
        o_da, (p_sg, p_da, w_o, w_gate, w_up, w_down) = _attention(
            slopes, qt, k, vt, row(lam_q1[l]), row(lam_k1[l]), row(lam_q2[l]), row(lam_k2[l]),
            row(subln_g[l]),
            [w_proj_sg[l], w_proj_da[l], w_out[l], w_ffn_gate[l], w_ffn_up[l], w_ffn_down[l]])
        sg_bias = jnp.repeat(sg_b[l].T.astype(F32), SG_GROUP_DIM, axis=1)
        h = _merge(x, uv, o_da, gate_pre, row(b_gate[l]), row(sg_ln_g[l]), row(sg_ln_b[l]),
                   sg_w[l].astype(BF16), sg_bias, p_sg, p_da, w_o)
        x = _ffn(h, row(norm2_g[l]), w_gate, w_up, w_down)
    return x
```

```python
import functools
import math

import numpy as np
import jax
import jax.numpy as jnp
from jax import lax
from jax.experimental import pallas as pl
from jax.experimental.pallas import tpu as pltpu

D_MODEL = 1024
SG_GROUPS = 8
SG_GROUP_DIM = 64
SG_WIDTH = SG_GROUPS * SG_GROUP_DIM
CHUNK = 128
DA_HEADS = 8
DA_HEAD_DIM = 64
DA_V_DIM = 2 * DA_HEAD_DIM
DA_WIDTH = DA_HEADS * DA_V_DIM
EPS = 1e-6
LAMBDA_INIT = 0.8 - 0.6 * math.exp(-0.3 * 0)
LOG2E = math.log2(math.e)

BF16 = jnp.bfloat16
F32 = jnp.float32

LANES = 128
BF16_SUBLANES = 16
MXU_DIM = 256
VMEM_LIMIT_BYTES = 56 * 1024 * 1024

TM_PROJ = 512
TQ = 512
TK = 512
POS_SPLIT = 64
N_AUG = 15
SAFE_EXP2_RANGE = 60.0
KMAX_MARGIN = 1.01
BLOCKS_PER_REGION = 2
assert TQ == TK and N_AUG <= BF16_SUBLANES
TM_MERGE = 512
TM_FFN = 1024
FF_CHUNK = 256


def _dot(a, b):
    return jnp.dot(a, b, preferred_element_type=F32)


def _const_spec(shape):
    n = len(shape)
    return pl.BlockSpec(shape, lambda *_: (0,) * n, pipeline_mode=pl.Buffered(1))


def _group_ones(n, value=1.0):
    r = lax.broadcasted_iota(jnp.int32, (n, n), 0) // DA_HEAD_DIM
    c = lax.broadcasted_iota(jnp.int32, (n, n), 1) // DA_HEAD_DIM
    return jnp.where(r == c, value, 0.0).astype(BF16)


def _inproj_kernel(x_ref, n1g_ref, w_ref, qg_ref, kg_ref,
                   uv_ref, qt_ref, k_ref, vt_ref, gate_ref, xn_ref):
    x = x_ref[...]
    ms = jnp.mean(x * x, axis=-1, keepdims=True)
    xn_ref[...] = (x * lax.rsqrt(ms + EPS) * n1g_ref[...]).astype(BF16)
    averager = _group_ones(MXU_DIM, 1.0 / DA_HEAD_DIM)

    def unit_rms(p, gain_ref):
        sq = (p * p).astype(BF16)
        mean_sq = jnp.concatenate([_dot(sq[:, i * MXU_DIM:(i + 1) * MXU_DIM], averager)
                                   for i in range(p.shape[1] // MXU_DIM)], axis=1)
        return p * lax.rsqrt(mean_sq + EPS) * gain_ref[...]

    def spatial(p):
        uv_ref[...] = p.astype(BF16)

    def query(p):
        tm = p.shape[0]
        first = lax.broadcasted_iota(jnp.int32, (DA_V_DIM, tm), 0) < DA_HEAD_DIM
        gain = jnp.concatenate([qg_ref[...]] * (tm // LANES), axis=1)
        for h in range(DA_HEADS):
            pt = p[:, h * DA_V_DIM:(h + 1) * DA_V_DIM].T
            sq = pt * pt
            r1 = lax.rsqrt(jnp.mean(sq[:DA_HEAD_DIM], axis=0, keepdims=True) + EPS)
            r2 = lax.rsqrt(jnp.mean(sq[DA_HEAD_DIM:], axis=0, keepdims=True) + EPS)
            qt_ref[h] = (pt * jnp.where(first, r1, r2) * gain).astype(BF16)

    def key(p):
        kn = unit_rms(p, kg_ref).astype(BF16)
        for h in range(DA_HEADS):
            k_ref[h] = kn[:, h * DA_V_DIM:(h + 1) * DA_V_DIM]

    def value(p):
        for h in range(DA_HEADS):
            vt_ref[h] = p[:, h * DA_V_DIM:(h + 1) * DA_V_DIM].T.astype(BF16)

    def gate_sg(p):
        gate_ref[:, :D_MODEL] = p.astype(BF16)

    def gate_da(p):
        gate_ref[:, D_MODEL:] = p.astype(BF16)

    col = 0
    for width, epilogue in ((2 * SG_WIDTH, spatial), (DA_WIDTH, query), (DA_WIDTH, key), (DA_WIDTH, value),
                            (D_MODEL, gate_sg), (D_MODEL, gate_da)):
        epilogue(_dot(xn_ref[...], w_ref[:, col:col + width]))
        col += width


def _in_projection(x, n1g, w_in, qg, kg):
    B, S, D = x.shape
    tm = TM_PROJ
    n_t = S // tm
    n_kc = S // TK
    sub = TK // tm
    in_cols = w_in.shape[1]
    tok = lambda width: pl.BlockSpec((None, tm, width), lambda b, i: (b, i, 0))
    out_shape = (
        jax.ShapeDtypeStruct((B, S, 2 * SG_WIDTH), BF16),
        jax.ShapeDtypeStruct((B, DA_HEADS, S // TQ, DA_V_DIM, TQ), BF16),
        jax.ShapeDtypeStruct((B, DA_HEADS, S, DA_V_DIM), BF16),
        jax.ShapeDtypeStruct((B, DA_HEADS, n_kc, DA_V_DIM, TK), BF16),
        jax.ShapeDtypeStruct((B, S, 2 * D_MODEL), BF16),
    )
    out_specs = (
        tok(2 * SG_WIDTH),
        pl.BlockSpec((None, DA_HEADS, None, DA_V_DIM, tm), lambda b, i: (b, 0, i // sub, 0, i % sub)),
        pl.BlockSpec((None, DA_HEADS, tm, DA_V_DIM), lambda b, i: (b, 0, i, 0)),
        pl.BlockSpec((None, DA_HEADS, None, DA_V_DIM, tm), lambda b, i: (b, 0, i // sub, 0, i % sub)),
        tok(2 * D_MODEL),
    )
    in_specs = [
        tok(D),
        _const_spec((1, D)),
        _const_spec((D, in_cols)),
        _const_spec((DA_V_DIM, LANES)),
        _const_spec((1, DA_WIDTH)),
    ]
    return pl.pallas_call(
        _inproj_kernel,
        grid=(B, n_t),
        in_specs=in_specs,
        out_specs=out_specs,
        out_shape=out_shape,
        scratch_shapes=[pltpu.VMEM((tm, D), BF16)],
        compiler_params=pltpu.CompilerParams(vmem_limit_bytes=VMEM_LIMIT_BYTES),
        name="in_projection",
    )(x, n1g, w_in, qg, kg)


def _split3(x):
    hi = x.astype(BF16).astype(F32)
    r = x - hi
    mid = r.astype(BF16).astype(F32)
    lo = (r - mid).astype(BF16).astype(F32)
    return hi, mid, lo


def _alibi_constants(slopes, seq, tq):
    parts = np.stack(_split3(np.asarray(slopes, np.float32)), axis=0)
    idx = np.arange(LANES, dtype=np.int32)
    part_of = parts[idx % 3].T
    kpos = np.arange(seq, dtype=np.int32)[None, :, None]
    a = idx[None, None, :]
    kcol = np.where(a < 3, -1.0, np.where(a < 9, -part_of[:, None, :],
           np.where(a < 12, ((kpos // POS_SPLIT) * POS_SPLIT).astype(np.float32),
           np.where(a < N_AUG, (kpos % POS_SPLIT).astype(np.float32), 0.0))))
    qloc = (np.arange(2 * tq, dtype=np.int32) % tq)[None, None, :]
    a = np.arange(BF16_SUBLANES, dtype=np.int32)[None, :, None]
    qrow = np.where((a >= 3) & (a < 6), ((qloc // POS_SPLIT) * POS_SPLIT).astype(np.float32),
           np.where((a >= 6) & (a < 9), (qloc % POS_SPLIT).astype(np.float32),
           np.where((a >= 9) & (a < N_AUG), part_of[:, :BF16_SUBLANES, None], 0.0)))
    return kcol.astype(BF16), qrow.astype(np.float32)


def _attn_kernel(slopes_ref, qt_ref, k_ref, kcol_ref, qrow_ref, dist_ref, vt_ref,
                 lq1_ref, lk1_ref, lq2_ref, lk2_ref, sg_ref, *rest, n_weights):
    w_f32 = rest[:n_weights]
    o_ref = rest[n_weights]
    w_bf16 = rest[n_weights + 1:2 * n_weights + 1]
    qmain_ref, qaug_ref, acc_ref, kmax_ref, fast_ref = rest[2 * n_weights + 1:]
    for src, dst in zip(w_f32, w_bf16):
        dst[...] = src[...].astype(BF16)

    h = pl.program_id(1)
    n_q, dv, tq = qt_ref.shape
    n_kc, _, tk = vt_ref.shape
    slope = slopes_ref[h]

    def rows_of(index, size):
        start = index * size
        return pl.ds(start if isinstance(start, int) else pl.multiple_of(start, size), size)

    def k_chunk(j):
        return k_ref[rows_of(j, tk), :]

    @pl.when((pl.program_id(0) == 0) & (h == 0))
    def _():
        qaug_ref[:, :, BF16_SUBLANES:, :] = jnp.zeros(
            (qaug_ref.shape[0], 3, LANES - BF16_SUBLANES, 2 * tq), BF16)

    ones_bd = _group_ones(LANES)
    best = jnp.zeros((1, LANES), F32)
    for c in range(n_kc):
        kf = k_ref[c * tk:(c + 1) * tk, :].astype(F32)
        best = jnp.maximum(best, jnp.max(_dot((kf * kf).astype(BF16), ones_bd), axis=0, keepdims=True))
    kmax_ref[0:1, :] = jnp.sqrt(best) * KMAX_MARGIN
    lam = (jnp.exp(jnp.sum(lq1_ref[...] * lk1_ref[...], axis=-1, keepdims=True))
           - jnp.exp(jnp.sum(lq2_ref[...] * lk2_ref[...], axis=-1, keepdims=True)) + LAMBDA_INIT)

    def prepare(blk, slot):
        qt = qt_ref[blk]
        sub = lax.broadcasted_iota(jnp.int32, (dv, tq), 0)
        zero = jnp.zeros_like(qt)
        qmain_ref[slot, :, 0:tq] = jnp.where(sub < DA_HEAD_DIM, qt, zero)
        qmain_ref[slot, :, tq:2 * tq] = jnp.where(sub < DA_HEAD_DIM, zero, qt)
        qsq = qt.astype(F32) * qt.astype(F32)
        n1 = jnp.sqrt(jnp.sum(qsq[:DA_HEAD_DIM], axis=0, keepdims=True))
        n2 = jnp.sqrt(jnp.sum(qsq[DA_HEAD_DIM:], axis=0, keepdims=True))
        m = jnp.concatenate([n1 * kmax_ref[0:1, 0:1], n2 * kmax_ref[0:1, DA_HEAD_DIM:DA_HEAD_DIM + 1]], axis=1)
        a = lax.broadcasted_iota(jnp.int32, (BF16_SUBLANES, 2 * tq), 0)
        positional = qrow_ref[...] + jnp.where((a >= 3) & (a < 6), jnp.asarray(blk * tq, F32), 0.0)
        m_hi, m_mid, m_lo = _split3(m)
        m_rows = jnp.where(a == 0, m_hi, jnp.where(a == 1, m_mid, m_lo))
        for variant, tail in enumerate((positional, -positional, jnp.zeros_like(positional))):
            qaug_ref[slot, variant, 0:BF16_SUBLANES, :] = jnp.where(a < 3, m_rows, tail).astype(BF16)
        fast_ref[slot] = jnp.where(jnp.max(m) <= SAFE_EXP2_RANGE, 1, 0)

    def finish(blk, slot):
        o = acc_ref[slot, 0:dv, :] / acc_ref[slot, dv:dv + 1, :]
        o = o[:, :tq] - lam * o[:, tq:]
        ms = jnp.mean(o * o, axis=0, keepdims=True)
        gain = jnp.concatenate([sg_ref[...]] * (tq // LANES), axis=1) * (1.0 - LAMBDA_INIT)
        o_ref[blk] = (o * lax.rsqrt(ms + EPS) * gain).astype(BF16)

    def scores_fast(blk, slot):
        ones_row = jnp.where(lax.broadcasted_iota(jnp.int32, (BF16_SUBLANES, tk), 0) == 0, 1.0, 0.0).astype(BF16)

        def chunk(j, variant, bias):
            k_aug = jnp.concatenate([k_chunk(j), kcol_ref[rows_of(j, tk), :]], axis=1)
            e = _dot(k_aug, jnp.concatenate([qmain_ref[slot], qaug_ref[slot, variant]], axis=0))
            if bias is not None:
                e = e - bias
            p = jnp.exp2(e).astype(BF16)
            return _dot(jnp.concatenate([vt_ref[j], ones_row], axis=0), p)

        diag_bias = dist_ref[...] * slope
        acc = chunk(blk, 2, jnp.concatenate([diag_bias, diag_bias], axis=1))
        for t in range(n_kc - 1):
            j = t + jnp.where(t >= blk, 1, 0)
            acc = acc + chunk(j, jnp.where(j > blk, 1, 0), None)
        acc_ref[slot] = acc

    def scores_online(blk, slot):
        signed =(lax.broadcasted_iota(jnp.int32, (tk, 2 * tq), 1) % tq
                  - lax.broadcasted_iota(jnp.int32, (tk, 2 * tq), 0)).astype(F32) * slope

        def body(j, carry):
            mx, l, acc = carry
            off = jnp.asarray(blk * tq - j * tk, F32) * slope
            s = _dot(k_chunk(j), qmain_ref[slot]) - jnp.abs(signed + off)
            m_new = jnp.maximum(mx, jnp.max(s, axis=0, keepdims=True))
            alpha = jnp.exp2(mx - m_new)
            p = jnp.exp2(s - m_new)
            l = alpha * l + jnp.sum(p, axis=0, keepdims=True)
            acc = alpha * acc + _dot(vt_ref[j], p.astype(BF16))
            return m_new, l, acc

        m0 = jnp.full((1, 2 * tq), -jnp.inf, F32)
        l0 = jnp.zeros((1, 2 * tq), F32)
        a0 = jnp.zeros((dv, 2 * tq), F32)
        _, l, acc = lax.fori_loop(0, n_kc, body, (m0, l0, a0))
        acc_ref[slot, 0:dv, :] = acc
        acc_ref[slot, dv:dv + BF16_SUBLANES, :] = jnp.broadcast_to(l, (BF16_SUBLANES, 2 * tq))

    group = BLOCKS_PER_REGION

    def region(first_blk, cur, oth):
        blks = [first_blk + g for g in range(group)]
        prev = [jnp.maximum(b - group, g) for g, b in enumerate(blks)]
        nxt = [jnp.minimum(b + group, n_q - group + g) for g, b in enumerate(blks)]
        fast = fast_ref[cur[0]] == 1
        for s in cur[1:]:
            fast = fast & (fast_ref[s] == 1)

        def body(scores):
            for g in range(group):
                finish(prev[g], oth[g])
            for g in range(group):
                scores(blks[g], cur[g])
            for g in range(group):
                prepare(nxt[g], oth[g])

        pl.when(fast)(lambda: body(scores_fast))
        pl.when(jnp.logical_not(fast))(lambda: body(scores_online))

    slots_a = tuple(range(group))
    slots_b = tuple(range(group, 2 * group))
    for g in range(group):
        acc_ref[slots_b[g]] = jnp.ones(acc_ref.shape[1:], F32)
        prepare(g, slots_a[g])

    def two_regions(u, carry):
        region(2 * group * u, slots_a, slots_b)
        region(2 * group * u + group, slots_b, slots_a)
        return carry

    lax.fori_loop(0, n_q // (2 * group), two_regions, 0)
    for g in range(group):
        finish(n_q - group + g, slots_b[g])


def _attention(slopes, qt, k, vt, lq1, lk1, lq2, lk2, subln_g, later_weights):
    B, H, n_q, dv, tq = qt.shape
    n_kc = vt.shape[2]
    S = k.shape[2]
    n_slots = 2 * BLOCKS_PER_REGION
    assert n_q % n_slots == 0
    steps = B * H
    slabs = [w.reshape(steps, w.shape[0] // steps, w.shape[1]) for w in later_weights]
    slab_spec = lambda s: pl.BlockSpec((None,) + s.shape[1:], lambda b, h, *_: (b * H + h, 0, 0))
    kcol, qrow = _alibi_constants(slopes, S, tq)
    dist = np.abs(np.arange(TK, dtype=np.int32)[:, None]
                  - np.arange(tq, dtype=np.int32)[None, :]).astype(np.float32)
    slopes = np.asarray(slopes, np.float32)
    small = lambda n: pl.BlockSpec((1, n), lambda b, h, *_: (0, 0))
    grid_spec = pltpu.PrefetchScalarGridSpec(
        num_scalar_prefetch=1,
        grid=(B, H),
        in_specs=[
            pl.BlockSpec((None, None, n_q, dv, tq), lambda b, h, *_: (b, h, 0, 0, 0)),
            pl.BlockSpec((None, None, S, dv), lambda b, h, *_: (b, h, 0, 0)),
            pl.BlockSpec((None, S, LANES), lambda b, h, *_: (h, 0, 0)),
            pl.BlockSpec((None, BF16_SUBLANES, 2 * tq), lambda b, h, *_: (h, 0, 0)),
            pl.BlockSpec((TK, tq), lambda b, h, *_: (0, 0)),
            pl.BlockSpec((None, None, n_kc, dv, TK), lambda b, h, *_: (b, h, 0, 0, 0)),
            small(DA_HEAD_DIM), small(DA_HEAD_DIM), small(DA_HEAD_DIM), small(DA_HEAD_DIM),
            pl.BlockSpec((dv, LANES), lambda b, h, *_: (0, 0)),
        ] + [slab_spec(s) for s in slabs],
        out_specs=[pl.BlockSpec((None, None, n_q, dv, tq), lambda b, h, *_: (b, h, 0, 0, 0))]
                  + [slab_spec(s) for s in slabs],
        scratch_shapes=[
            pltpu.VMEM((n_slots, dv, 2 * tq), BF16),
            pltpu.VMEM((n_slots, 3, LANES, 2 * tq), BF16),
            pltpu.VMEM((n_slots, dv + BF16_SUBLANES, 2 * tq), F32),
            pltpu.VMEM((8, LANES), F32),
            pltpu.SMEM((n_slots,), jnp.int32),
        ],
    )
    o_da, *cast = pl.pallas_call(
        functools.partial(_attn_kernel, n_weights=len(slabs)),
        grid_spec=grid_spec,
        out_shape=[jax.ShapeDtypeStruct((B, H, n_q, dv, tq), BF16)]
                  + [jax.ShapeDtypeStruct(s.shape, BF16) for s in slabs],
        compiler_params=pltpu.CompilerParams(vmem_limit_bytes=VMEM_LIMIT_BYTES,
                                             dimension_semantics=("arbitrary", "arbitrary")),
        name="diff_attention",
    )(slopes, qt, k, kcol, qrow, dist, vt, lq1, lk1, lq2, lk2, subln_g, *slabs)
    return o_da, [c.reshape(w.shape) for c, w in zip(cast, later_weights)]


def _merge_kernel(x_ref, uv_ref, oda_ref, gate_ref, bg_ref, lng_ref, lnb_ref, sw_ref, sb_ref,
                  psg_ref, pda_ref, wo_ref, h_ref, vn_ref):
    tm = x_ref.shape[0]
    n_chunks = tm // CHUNK

    oda_t = oda_ref[...].reshape(DA_WIDTH, tm)
    y_da = lax.dot_general(oda_t, pda_ref[...], (((0,), (0,)), ((), ())), preferred_element_type=F32)

    gu = jax.nn.gelu(uv_ref[:, :SG_WIDTH].astype(F32))
    gv = jax.nn.gelu(uv_ref[:, SG_WIDTH:].astype(F32))
    mu = jnp.mean(gv, axis=-1, keepdims=True)
    cen = gv - mu
    var = jnp.mean(cen * cen, axis=-1, keepdims=True)
    vn_ref[...] = (cen * lax.rsqrt(var + EPS) * lng_ref[...] + lnb_ref[...]).astype(BF16)

    lane = lax.broadcasted_iota(jnp.int32, (CHUNK, tm), 1)
    first_group = (lane % LANES) < SG_GROUP_DIM
    mixed = []
    for pr in range(SG_GROUPS // 2):
        vp = jnp.concatenate([vn_ref[c * CHUNK:(c + 1) * CHUNK, pr * LANES:(pr + 1) * LANES]
                              for c in range(n_chunks)], axis=1)
        mixed.append(jnp.where(first_group, _dot(sw_ref[2 * pr], vp), _dot(sw_ref[2 * pr + 1], vp)))
    bias = sb_ref[...]
    sv = jnp.concatenate(
        [jnp.concatenate([m[:, c * LANES:(c + 1) * LANES] for m in mixed], axis=1) + bias
         for c in range(n_chunks)], axis=0)
    o_sg = (gu * sv).astype(BF16)

    y_sg = _dot(o_sg, psg_ref[...])
    g_sg = jax.nn.sigmoid(gate_ref[:, :D_MODEL].astype(F32) + bg_ref[:, :D_MODEL])
    g_da = jax.nn.sigmoid(gate_ref[:, D_MODEL:].astype(F32) + bg_ref[:, D_MODEL:])
    z = (g_sg * y_sg + g_da * y_da).astype(BF16)
    h_ref[...] = x_ref[...] + _dot(z, wo_ref[...])


def _merge(x, uv, o_da, gate_pre, b_gate, ln_g, ln_b, sg_w, sg_bias, p_sg, p_da, w_out):
    B, S, D = x.shape
    tm = TM_MERGE
    tok = lambda width: pl.BlockSpec((None, tm, width), lambda b, i: (b, i, 0))
    return pl.pallas_call(
        _merge_kernel,
        grid=(B, S // tm),
        in_specs=[
            tok(D), tok(2 * SG_WIDTH),
            pl.BlockSpec((None, DA_HEADS, None, DA_V_DIM, tm), lambda b, i: (b, 0, i, 0, 0)),
            tok(2 * D_MODEL),
            _const_spec(b_gate.shape), _const_spec(ln_g.shape), _const_spec(ln_b.shape),
            _const_spec(sg_w.shape), _const_spec(sg_bias.shape),
            _const_spec(p_sg.shape), _const_spec(p_da.shape), _const_spec(w_out.shape),
        ],
        out_specs=tok(D),
        out_shape=jax.ShapeDtypeStruct((B, S, D), F32),
        scratch_shapes=[pltpu.VMEM((tm, SG_WIDTH), BF16)],
        compiler_params=pltpu.CompilerParams(vmem_limit_bytes=VMEM_LIMIT_BYTES),
        name="merge",
    )(x, uv, o_da, gate_pre, b_gate, ln_g, ln_b, sg_w, sg_bias, p_sg, p_da, w_out)


def _ffn_kernel(h_ref, g_ref, wg_ref, wu_ref, wd_ref, o_ref, hn_ref):
    hres = h_ref[...]
    ms = jnp.mean(hres * hres, axis=-1, keepdims=True)
    hn_ref[...] = (hres * lax.rsqrt(ms + EPS) * g_ref[...]).astype(BF16)
    d_ff = wg_ref.shape[1]
    acc = hres
    for c in range(d_ff // FF_CHUNK):
        sl = slice(c * FF_CHUNK, (c + 1) * FF_CHUNK)
        a = jax.nn.silu(_dot(hn_ref[...], wg_ref[:, sl])) * _dot(hn_ref[...], wu_ref[:, sl])
        acc = acc + _dot(a.astype(BF16), wd_ref[sl, :])
    o_ref[...] = acc


def _ffn(h, g, w_gate, w_up, w_down):
    B, S, D = h.shape
    tm = TM_FFN
    tok = pl.BlockSpec((None, tm, D), lambda b, i: (b, i, 0))
    return pl.pallas_call(
        _ffn_kernel,
        grid=(B, S // tm),
        in_specs=[tok, _const_spec((1, D)), _const_spec(w_gate.shape), _const_spec(w_up.shape),
                  _const_spec(w_down.shape)],
        out_specs=tok,
        out_shape=jax.ShapeDtypeStruct((B, S, D), F32),
        scratch_shapes=[pltpu.VMEM((tm, D), BF16)],
        compiler_params=pltpu.CompilerParams(vmem_limit_bytes=VMEM_LIMIT_BYTES),
        name="swiglu_ffn",
    )(h, g, w_gate, w_up, w_down)


def kernel(x, norm1_g, w_in, b_gate, sg_ln_g, sg_ln_b, sg_w, sg_b, q_norm_g, k_norm_g, lam_q1, lam_k1, lam_q2, lam_k2, subln_g, w_proj_sg, w_proj_da, w_out, norm2_g, w_ffn_gate, w_ffn_up, w_ffn_down):
    depth = w_in.shape[0]
    row = lambda a: a.reshape(1, -1).astype(F32)
    n_rep = DA_WIDTH // DA_HEAD_DIM
    slopes = [2.0 ** (-8.0 * (i + 1) / DA_HEADS) * LOG2E for i in range(DA_HEADS)]
    for l in range(depth):
        qg = jnp.broadcast_to(jnp.tile(q_norm_g[l].astype(F32), 2)[:, None] * (LOG2E / math.sqrt(DA_HEAD_DIM)),
                              (DA_V_DIM, LANES))
        kg = row(jnp.tile(k_norm_g[l], n_rep))
        uv, qt, k, vt, gate_pre = _in_projection(x, row(norm1_g[l]), w_in[l].astype(BF16), qg, kg)
        o_da, (p_sg, p_da, w_o, w_gate, w_up, w_down) = _attention(
            slopes, qt, k, vt, row(lam_q1[l]), row(lam_k1[l]), row(lam_q2[l]), row(lam_k2[l]),
            jnp.broadcast_to(subln_g[l].astype(F32)[:, None], (DA_V_DIM, LANES)),
            [w_proj_sg[l], w_proj_da[l], w_out[l], w_ffn_gate[l], w_ffn_up[l], w_ffn_down[l]])
        sg_bias = jnp.repeat(sg_b[l].T.astype(F32), SG_GROUP_DIM, axis=1)
        h = _merge(x, uv, o_da, gate_pre, row(b_gate[l]), row(sg_ln_g[l]), row(sg_ln_b[l]),
                   sg_w[l].astype(BF16), sg_bias, p_sg, p_da, w_o)
        x = _ffn(h, row(norm2_g[l]), w_gate, w_up, w_down)
    return x
```
